```python
import jax, jax.numpy as jnp
from jax import lax
import numpy as np

D_MODEL = 2048
BATCH = 4
SEQ = 4096
DEPTH = 2

HEAD_DIM = 128
MLA_HEADS = 4
MLA_Q_LORA = 512
MLA_KV_LORA = 256
MLA_NOPE = 128
MLA_ROPE = 64
MLA_V = 128
FOX_HEADS = 4
NSA_HEADS = 8
NSA_KV_GROUPS = 2
NSA_GQA = NSA_HEADS // NSA_KV_GROUPS
NSA_CMP_LEN = 32
NSA_CMP_STRIDE = 16
NSA_CMP_HIDDEN = 512
NSA_SLC_BLOCK = 64
NSA_TOPK = 16
NSA_WINDOW = 512
NSA_SLC_Q_BLOCK = 64
W_MLA = MLA_HEADS * MLA_V
W_FOX = FOX_HEADS * HEAD_DIM
W_NSA = NSA_HEADS * HEAD_DIM
D_MIX = W_MLA + W_FOX + W_NSA
NSA_KV = NSA_KV_GROUPS * HEAD_DIM
IN_SPLITS = (MLA_Q_LORA, MLA_KV_LORA, MLA_ROPE,
             W_FOX, W_FOX, W_FOX, FOX_HEADS,
             W_NSA, NSA_KV, NSA_KV, NSA_KV, NSA_KV, NSA_KV, NSA_KV, 3 * NSA_HEADS)
IN_COLS = sum(IN_SPLITS)
Q_BLOCK = 128
ROPE_THETA = 500000.0
PARTIAL_ROT = HEAD_DIM // 4
D_FF = 4 * D_MODEL
MAX_POS_OFFSET = 1024
EPS = 1e-6
NEG = -1e30
BIG = 1e30

kernel_name = 'hybrid_mla_fox_nsa_block'

f32 = jnp.float32


def rms_norm(x, gain):
    x32 = x.astype(f32)
    y = x32 * lax.rsqrt(jnp.mean(x32 * x32, axis=-1, keepdims=True) + EPS)
    return (y * gain.astype(f32)).astype(x.dtype)


def rotary(x, positions, rot_dim):
    half = rot_dim // 2
    inv_freq = ROPE_THETA ** (-jnp.arange(half, dtype=f32) / half)
    ang = positions.astype(f32)[..., None] * inv_freq
    ang = ang.reshape(ang.shape[:2] + (1,) * (x.ndim - 3) + (half,))
    cos, sin = jnp.cos(ang), jnp.sin(ang)
    xf = x.astype(f32)
    x1, x2, rest = xf[..., :half], xf[..., half:rot_dim], xf[..., rot_dim:]
    out = jnp.concatenate([x1 * cos - x2 * sin, x2 * cos + x1 * sin, rest], axis=-1)
    return out.astype(x.dtype)


def causal_block_attention(q, k, v, scale, decay=None):
    B, S, H, Dk = q.shape
    nb = S // Q_BLOCK
    key_pos = jnp.arange(S)
    qb = q.reshape(B, nb, Q_BLOCK, H, Dk).swapaxes(0, 1)
    xs = (jnp.arange(nb), qb)
    if decay is not None:
        dk_all = decay.transpose(0, 2, 1)
        xs = xs + (decay.reshape(B, nb, Q_BLOCK, H).swapaxes(0, 1),)

    def block(args):
        i, q_blk = args[0], args[1]
        s = jnp.einsum('bqhd,bkhd->bhqk', q_blk, k).astype(f32) * scale
        if decay is not None:
            s = s + args[2].transpose(0, 2, 1)[..., None] - dk_all[:, :, None, :]
        qpos = i * Q_BLOCK + jnp.arange(Q_BLOCK)
        s = jnp.where(key_pos[None, :] <= qpos[:, None], s, NEG)
        p = jax.nn.softmax(s, axis=-1).astype(v.dtype)
        return jnp.einsum('bhqk,bkhd->bqhd', p, v)

    o = lax.map(block, xs)
    return o.swapaxes(0, 1).reshape(B, S, H, v.shape[-1])


def mla_mixer(cq, ckv, kr, positions, q_norm, w_uq, kv_norm, w_ukv):
    B, S, _ = cq.shape
    q = (rms_norm(cq, q_norm) @ w_uq).reshape(B, S, MLA_HEADS, MLA_NOPE + MLA_ROPE)
    kv = (rms_norm(ckv, kv_norm) @ w_ukv).reshape(B, S, MLA_HEADS, MLA_NOPE + MLA_V)
    q_rope = rotary(q[..., MLA_NOPE:], positions, MLA_ROPE)
    k_rope = rotary(kr, positions, MLA_ROPE)
    q_full = jnp.concatenate([q[..., :MLA_NOPE], q_rope], axis=-1)
    k_full = jnp.concatenate(
        [kv[..., :MLA_NOPE], jnp.broadcast_to(k_rope[:, :, None, :], (B, S, MLA_HEADS, MLA_ROPE))], axis=-1)
    o = causal_block_attention(q_full, k_full, kv[..., MLA_NOPE:], (MLA_NOPE + MLA_ROPE) ** -0.5)
    return o.reshape(B, S, W_MLA)


def fox_mixer(q, k, v, f_logit, b_f):
    B, S, _ = q.shape
    shp = (B, S, FOX_HEADS, HEAD_DIM)
    log_f = jax.nn.log_sigmoid(f_logit.astype(f32) + b_f.astype(f32))
    cum = lax.cumsum(log_f, axis=1)
    o = causal_block_attention(q.reshape(shp), k.reshape(shp), v.reshape(shp), HEAD_DIM ** -0.5, decay=cum)
    return o.reshape(B, S, W_FOX)


def nsa_compress(tok, pos_emb, w1, w2):
    B, S, G, D = tok.shape
    n_cmp = (S - NSA_CMP_LEN) // NSA_CMP_STRIDE + 1
    idx = np.arange(n_cmp)[:, None] * NSA_CMP_STRIDE + np.arange(NSA_CMP_LEN)[None, :]
    blk = tok[:, idx] + pos_emb[None, None, :, None, :]
    blk = blk.transpose(0, 1, 3, 2, 4).reshape(B, n_cmp, G, NSA_CMP_LEN * D)
    return jax.nn.gelu(blk @ w1) @ w2


def nsa_cmp_slc(q, k_cmp, v_cmp, k_slc, v_slc):
    B, S, G, Hg, D = q.shape
    n_cmp = k_cmp.shape[1]
    n_slc = S // NSA_SLC_BLOCK
    top_k = min(NSA_TOPK, n_slc)
    scale = D ** -0.5
    c_start = np.arange(n_cmp) * NSA_CMP_STRIDE
    cmp_end = jnp.asarray(c_start + NSA_CMP_LEN - 1)
    s_start = np.arange(n_slc) * NSA_SLC_BLOCK
    ov = np.minimum(c_start[:, None] + NSA_CMP_LEN, s_start[None, :] + NSA_SLC_BLOCK) - np.maximum(c_start[:, None], s_start[None, :])
    overlap = jnp.asarray((np.clip(ov, 0, None) / NSA_CMP_LEN).astype(np.float32))
    k_sb = k_slc.reshape(B, n_slc, NSA_SLC_BLOCK, G, D).transpose(0, 3, 1, 2, 4)
    v_sb = v_slc.reshape(B, n_slc, NSA_SLC_BLOCK, G, D).transpose(0, 3, 1, 2, 4)
    nq = S // NSA_SLC_Q_BLOCK
    qc = q.reshape(B, nq, NSA_SLC_Q_BLOCK, G, Hg, D).swapaxes(0, 1)
    b_ix = jnp.arange(B)[:, None, None, None]
    g_ix = jnp.arange(G)[None, :, None, None]
    j = jnp.arange(n_slc)

    def chunk(args):
        i, q_c = args
        t = i * NSA_SLC_Q_BLOCK + jnp.arange(NSA_SLC_Q_BLOCK)
        s = jnp.einsum('bqghd,bngd->bghqn', q_c, k_cmp).astype(f32) * scale
        valid = cmp_end[None, :] <= t[:, None]
        s = jnp.where(valid, s, NEG)
        p = jax.nn.softmax(s, axis=-1) * jnp.any(valid, axis=-1)[:, None].astype(f32)
        o_cmp = jnp.einsum('bghqn,bngd->bqghd', p.astype(v_cmp.dtype), v_cmp)
        imp = jnp.einsum('bghqn,nm->bgqm', p, overlap)
        cur = t // NSA_SLC_BLOCK
        forced = (j[None, :] == 0) | (j[None, :] == cur[:, None]) | (j[None, :] == cur[:, None] - 1)
        future = j[None, :] > cur[:, None]
        imp = jnp.where(forced, BIG, jnp.where(future, NEG, imp))
        _, sel = lax.top_k(imp, top_k)
        sel_ok = sel <= cur[None, None, :, None]
        k_g = k_sb[b_ix, g_ix, sel]
        v_g = v_sb[b_ix, g_ix, sel]
        tok = sel[..., None] * NSA_SLC_BLOCK + jnp.arange(NSA_SLC_BLOCK)
        m = (tok <= t[None, None, :, None, None]) & sel_ok[..., None]
        s2 = jnp.einsum('bqghd,bgqksd->bgqhks', q_c, k_g).astype(f32) * scale
        s2 = jnp.where(m[:, :, :, None], s2, NEG).reshape(B, G, NSA_SLC_Q_BLOCK, Hg, top_k * NSA_SLC_BLOCK)
        p2 = jax.nn.softmax(s2, axis=-1).astype(v_g.dtype).reshape(B, G, NSA_SLC_Q_BLOCK, Hg, top_k, NSA_SLC_BLOCK)
        o_slc = jnp.einsum('bgqhks,bgqksd->bqghd', p2, v_g)
        return o_cmp, o_slc

    o_cmp, o_slc = lax.map(chunk, (jnp.arange(nq), qc))
    return (o_cmp.swapaxes(0, 1).reshape(B, S, G, Hg, D),
            o_slc.swapaxes(0, 1).reshape(B, S, G, Hg, D))


def nsa_window(q, k, v):
    B, S, G, Hg, D = q.shape
    span = NSA_WINDOW + Q_BLOCK
    kp = jnp.pad(k, ((0, 0), (NSA_WINDOW, 0), (0, 0), (0, 0)))
    vp = jnp.pad(v, ((0, 0), (NSA_WINDOW, 0), (0, 0), (0, 0)))
    nb = S // Q_BLOCK
    qb = q.reshape(B, nb, Q_BLOCK, G, Hg, D).swapaxes(0, 1)

    def block(args):
        i, q_blk = args
        start = i * Q_BLOCK
        k_blk = lax.dynamic_slice_in_dim(kp, start, span, axis=1)
        v_blk = lax.dynamic_slice_in_dim(vp, start, span, axis=1)
        t = start + jnp.arange(Q_BLOCK)
        s_pos = start - NSA_WINDOW + jnp.arange(span)
        m = (s_pos[None, :] <= t[:, None]) & (t[:, None] - s_pos[None, :] < NSA_WINDOW) & (s_pos[None, :] >= 0)
        s = jnp.einsum('bqghd,bkgd->bghqk', q_blk, k_blk).astype(f32) * (D ** -0.5)
        s = jnp.where(m, s, NEG)
        p = jax.nn.softmax(s, axis=-1).astype(v_blk.dtype)
        return jnp.einsum('bghqk,bkgd->bqghd', p, v_blk)

    o = lax.map(block, (jnp.arange(nb), qb))
    return o.swapaxes(0, 1).reshape(B, S, G, Hg, D)


def nsa_mixer(q, kc, vc, ks, vs, kw, vw, gate_logit, positions, pos_k, pos_v, ck_w1, ck_w2, cv_w1, cv_w2):
    B, S, _ = q.shape
    kv_shape = (B, S, NSA_KV_GROUPS, HEAD_DIM)
    q = rotary(q.reshape(B, S, NSA_KV_GROUPS, NSA_GQA, HEAD_DIM), positions, PARTIAL_ROT)
    kc = rotary(kc.reshape(kv_shape), positions, PARTIAL_ROT)
    ks = rotary(ks.reshape(kv_shape), positions, PARTIAL_ROT)
    kw = rotary(kw.reshape(kv_shape), positions, PARTIAL_ROT)
    k_cmp = nsa_compress(kc, pos_k, ck_w1, ck_w2)
    v_cmp = nsa_compress(vc.reshape(kv_shape), pos_v, cv_w1, cv_w2)
    o_cmp, o_slc = nsa_cmp_slc(q, k_cmp, v_cmp, ks, vs.reshape(kv_shape))
    o_win = nsa_window(q, kw, vw.reshape(kv_shape))
    g = jax.nn.sigmoid(gate_logit.astype(f32)).reshape(B, S, NSA_KV_GROUPS, NSA_GQA, 3).astype(q.dtype)
    o = g[..., 0:1] * o_cmp + g[..., 1:2] * o_slc + g[..., 2:3] * o_win
    return o.reshape(B, S, W_NSA)


def hybrid_layer(x, c_act, positions, ada_w, ada_b, norm_mix, norm_mlp, w_in, fox_b_f,
                 mla_q_norm, mla_w_uq, mla_kv_norm, mla_w_ukv,
                 nsa_pos_k, nsa_pos_v, nsa_cmp_k_w1, nsa_cmp_k_w2, nsa_cmp_v_w1, nsa_cmp_v_w2,
                 out_norm, w_out, mlp_w1, mlp_w2):
    mod = c_act @ ada_w + ada_b
    sh_a, sc_a, gt_a, sh_m, sc_m, gt_m = jnp.split(mod[:, None, :], 6, axis=-1)
    h = rms_norm(x, norm_mix) * (1.0 + sc_a) + sh_a
    split_points = np.cumsum(IN_SPLITS)[:-1].tolist()
    (cq, ckv, kr, fq, fk, fv, ff,
     nq, nkc, nvc, nks, nvs, nkw, nvw, ng) = jnp.split(h @ w_in, split_points, axis=-1)
    o_mla = mla_mixer(cq, ckv, kr, positions, mla_q_norm, mla_w_uq, mla_kv_norm, mla_w_ukv)
    o_fox = fox_mixer(fq, fk, fv, ff, fox_b_f)
    o_nsa = nsa_mixer(nq, nkc, nvc, nks, nvs, nkw, nvw, ng, positions,
                      nsa_pos_k, nsa_pos_v, nsa_cmp_k_w1, nsa_cmp_k_w2, nsa_cmp_v_w1, nsa_cmp_v_w2)
    g_mla, g_fox, g_nsa = jnp.split(out_norm, [W_MLA, W_MLA + W_FOX])
    o = jnp.concatenate([rms_norm(o_mla, g_mla), rms_norm(o_fox, g_fox), rms_norm(o_nsa, g_nsa)], axis=-1)
    x = x + gt_a * (o @ w_out)
    h = rms_norm(x, norm_mlp) * (1.0 + sc_m) + sh_m
    u = jnp.square(jax.nn.relu(h @ mlp_w1))
    return x + gt_m * (u @ mlp_w2)


def setup_inputs(seed: int = 0) -> dict:
    key = jax.random.key(seed)
    ks = iter(jax.random.split(key, 32))

    def dense(shape, fan_in, s=1.0):
        return jax.random.normal(next(ks), shape, f32) * (s * fan_in ** -0.5)

    def gain(shape):
        return 1.0 + 0.02 * jax.random.normal(next(ks), shape, f32)

    L = DEPTH
    x = jax.random.normal(next(ks), (BATCH, SEQ, D_MODEL), f32)
    c = jax.random.normal(next(ks), (BATCH, D_MODEL), f32)
    positions = (jax.random.randint(next(ks), (BATCH, 1), 0, MAX_POS_OFFSET, dtype=jnp.int32)
                 + jnp.arange(SEQ, dtype=jnp.int32)[None, :])
    return {
        'x': x,
        'c': c,
        'positions': positions,
        'ada_w': dense((L, D_MODEL, 6 * D_MODEL), D_MODEL),
        'ada_b': 0.02 * jax.random.normal(next(ks), (L, 6 * D_MODEL), f32),
        'norm_mix': gain((L, D_MODEL)),
        'norm_mlp': gain((L, D_MODEL)),
        'w_in': dense((L, D_MODEL, IN_COLS), D_MODEL),
        'fox_b_f': jax.random.uniform(next(ks), (L, FOX_HEADS), f32, minval=1.0, maxval=4.0),
        'mla_q_norm': gain((L, MLA_Q_LORA)),
        'mla_w_uq': dense((L, MLA_Q_LORA, MLA_HEADS * (MLA_NOPE + MLA_ROPE)), MLA_Q_LORA),
        'mla_kv_norm': gain((L, MLA_KV_LORA)),
        'mla_w_ukv': dense((L, MLA_KV_LORA, MLA_HEADS * (MLA_NOPE + MLA_V)), MLA_KV_LORA),
        'nsa_pos_k': 0.1 * jax.random.normal(next(ks), (L, NSA_CMP_LEN, HEAD_DIM), f32),
        'nsa_pos_v': 0.1 * jax.random.normal(next(ks), (L, NSA_CMP_LEN, HEAD_DIM), f32),
        'nsa_cmp_k_w1': dense((L, NSA_CMP_LEN * HEAD_DIM, NSA_CMP_HIDDEN), NSA_CMP_LEN * HEAD_DIM),
        'nsa_cmp_k_w2': dense((L, NSA_CMP_HIDDEN, HEAD_DIM), NSA_CMP_HIDDEN),
        'nsa_cmp_v_w1': dense((L, NSA_CMP_LEN * HEAD_DIM, NSA_CMP_HIDDEN), NSA_CMP_LEN * HEAD_DIM),
        'nsa_cmp_v_w2': dense((L, NSA_CMP_HIDDEN, HEAD_DIM), NSA_CMP_HIDDEN),
        'out_norm': gain((L, D_MIX)),
        'w_out': dense((L, D_MIX, D_MODEL), D_MIX),
        'mlp_w1': dense((L, D_MODEL, D_FF), D_MODEL),
        'mlp_w2': dense((L, D_FF, D_MODEL), D_FF),
        'final_norm': gain((D_MODEL,)),
    }


def reference(x, c, positions, ada_w, ada_b, norm_mix, norm_mlp, w_in, fox_b_f,
              mla_q_norm, mla_w_uq, mla_kv_norm, mla_w_ukv,
              nsa_pos_k, nsa_pos_v, nsa_cmp_k_w1, nsa_cmp_k_w2, nsa_cmp_v_w1, nsa_cmp_v_w2,
              out_norm, w_out, mlp_w1, mlp_w2, final_norm):
    c_act = jax.nn.silu(c)
    for l in range(DEPTH):
        x = hybrid_layer(x, c_act, positions, ada_w[l], ada_b[l], norm_mix[l], norm_mlp[l], w_in[l], fox_b_f[l],
                         mla_q_norm[l], mla_w_uq[l], mla_kv_norm[l], mla_w_ukv[l],
                         nsa_pos_k[l], nsa_pos_v[l], nsa_cmp_k_w1[l], nsa_cmp_k_w2[l],
                         nsa_cmp_v_w1[l], nsa_cmp_v_w2[l],
                         out_norm[l], w_out[l], mlp_w1[l], mlp_w2[l])
    return rms_norm(x, final_norm)
```

```python
import functools
import math

import numpy as np
import jax
import jax.numpy as jnp
from jax import lax
from jax.experimental import pallas as pl
from jax.experimental.pallas import tpu as pltpu

f32 = jnp.float32
_MXU_DTYPE = jnp.bfloat16

HEAD_DIM = 128
MLA_HEADS = 4
MLA_Q_LORA = 512
MLA_KV_LORA = 256
MLA_NOPE = 128
MLA_ROPE = 64
MLA_V = 128
FOX_HEADS = 4
NSA_HEADS = 8
NSA_KV_GROUPS = 2
NSA_GQA = NSA_HEADS // NSA_KV_GROUPS
NSA_CMP_LEN = 32
NSA_CMP_STRIDE = 16
NSA_CMP_HIDDEN = 512
NSA_SLC_BLOCK = 64
NSA_TOPK = 16
NSA_WINDOW = 512
W_MLA = MLA_HEADS * MLA_V
W_FOX = FOX_HEADS * HEAD_DIM
W_NSA = NSA_HEADS * HEAD_DIM
NSA_KV = NSA_KV_GROUPS * HEAD_DIM
ROPE_THETA = 500000.0
PARTIAL_ROT = HEAD_DIM // 4
EPS = 1e-6
NEG = -1e30
BIG = 1e30

LANES = 128
VMEM_LIMIT = 56 * 1024 * 1024

MLA_QK = 2 * LANES
COL_NQ = 0
COL_CQ = 1024
COL_FQ = 1536
COL_FK = 2048
COL_FV = 2560
COL_CKV = 3072
COL_NKC = 3328
COL_NVC = 3584
COL_NKS = 3840
COL_NVS = 4096
COL_NKW = 4352
COL_NVW = 4608
COL_KR = 4864
COL_FF = 4992
COL_NG = 5120
N_COLS = 5376


def _pick(n, cap, mult=LANES):
    if n <= cap:
        return n
    t = (cap // mult) * mult
    while t >= mult:
        if n % t == 0:
            return t
        t -= mult
    raise ValueError(f"no tile for {n} under {cap}")


def _mm(a, b):
    return jnp.dot(a, b, preferred_element_type=f32)


def _mm_nt(a, b):
    return lax.dot_general(a, b, (((1,), (1,)), ((), ())), preferred_element_type=f32)


def _split3(a):
    hi = a.astype(_MXU_DTYPE)
    r1 = a - hi.astype(f32)
    mid = r1.astype(_MXU_DTYPE)
    lo = (r1 - mid.astype(f32)).astype(_MXU_DTYPE)
    return hi, mid, lo


def _rms(x, gain):
    return x * lax.rsqrt(jnp.mean(x * x, axis=-1, keepdims=True) + EPS) * gain


def _cparams(sem):
    return pltpu.CompilerParams(dimension_semantics=sem, vmem_limit_bytes=VMEM_LIMIT)


def _ada_kernel(c_ref, w_ref, b_ref, o_ref):
    c = c_ref[...]
    ca = c * jax.nn.sigmoid(c)
    w = w_ref[0]
    c_hi, c_mid, _ = _split3(ca)
    w_hi, w_mid, _ = _split3(w)
    acc = _mm(c_hi, w_hi) + (_mm(c_mid, w_hi) + _mm(c_hi, w_mid))
    o_ref[0] = acc + b_ref[0]


def _ada_mod(c_pad, ada_w, ada_b):
    L, D, D6 = ada_w.shape
    tn = _pick(D6, 1024)
    return pl.pallas_call(
        _ada_kernel,
        grid=(L, D6 // tn),
        in_specs=[
            pl.BlockSpec((8, D), lambda l, j: (0, 0)),
            pl.BlockSpec((1, D, tn), lambda l, j: (l, 0, j)),
            pl.BlockSpec((1, 1, tn), lambda l, j: (l, 0, j)),
        ],
        out_specs=pl.BlockSpec((1, 8, tn), lambda l, j: (l, 0, j)),
        out_shape=jax.ShapeDtypeStruct((L, 8, D6), f32),
        compiler_params=_cparams(("arbitrary", "arbitrary")),
        name="ada_mod",
    )(c_pad, ada_w, ada_b.reshape(L, 1, D6))


def _proj_in_kernel(x_ref, sh_ref, sc_ref, g_ref, w_ref, o_ref, h_ref):
    @pl.when(pl.program_id(1) == 0)
    def _():
        h = _rms(x_ref[...], g_ref[...]) * (1.0 + sc_ref[0]) + sh_ref[0]
        h_ref[...] = h.astype(h_ref.dtype)

    o_ref[...] = _mm(h_ref[...], w_ref[...])


def _proj_in(x2, mod3, gain, w, S):
    N, D = x2.shape
    NC = w.shape[1]
    tm = _pick(S, 512, 8)
    tn = _pick(NC, 768)
    return pl.pallas_call(
        _proj_in_kernel,
        grid=(N // tm, NC // tn),
        in_specs=[
            pl.BlockSpec((tm, D), lambda i, j: (i, 0)),
            pl.BlockSpec((1, 1, D), lambda i, j: (i * tm // S, 0, 0)),
            pl.BlockSpec((1, 1, D), lambda i, j: (i * tm // S, 0, 1)),
            pl.BlockSpec((1, D), lambda i, j: (0, 0)),
            pl.BlockSpec((D, tn), lambda i, j: (0, j)),
        ],
        out_specs=pl.BlockSpec((tm, tn), lambda i, j: (i, j)),
        out_shape=jax.ShapeDtypeStruct((N, NC), f32),
        scratch_shapes=[pltpu.VMEM((tm, D), _MXU_DTYPE)],
        compiler_params=_cparams(("arbitrary", "arbitrary")),
        name="proj_in",
    )(x2, mod3, mod3, gain, w)


def _rot_tables(pos_ref, invf_ref, sign_ref):
    ang = pos_ref[...].astype(f32) * invf_ref[...]
    return jnp.cos(ang), jnp.sin(ang) * sign_ref[...]


def _rotate(x, cos_t, sin_t, half, period):
    lane = lax.broadcasted_iota(jnp.int32, x.shape, 1)
    partner = jnp.where((lane % period) < half,
                        pltpu.roll(x, LANES - half, 1),
                        pltpu.roll(x, half, 1))
    return x * cos_t + partner * sin_t


def _rope_consts(half, period, width):
    inv = ROPE_THETA ** (-jnp.arange(half, dtype=f32) / half)
    lane = np.arange(LANES)
    in_rot = ((lane % period) < 2 * half) & (lane < width)
    idx = (lane % period) % half
    invf = jnp.where(jnp.asarray(in_rot), inv[idx], 0.0).reshape(1, LANES).astype(f32)
    sign = np.where(in_rot, np.where((lane % period) < half, -1.0, 1.0), 0.0)
    return invf, jnp.asarray(sign, f32).reshape(1, LANES)


def _mla_up_kernel(cq_ref, ckv_ref, kr_ref, pos_ref, invf_ref, sign_ref, gq_ref, gkv_ref, wq_ref, wkv_ref,
                   q_ref, k_ref, v_ref, *, scale):
    cos_t, sin_t = _rot_tables(pos_ref, invf_ref, sign_ref)
    half = MLA_ROPE // 2
    qn = _rms(cq_ref[...], gq_ref[...]).astype(_MXU_DTYPE)
    q = _mm(qn, wq_ref[...]) * scale
    kvn = _rms(ckv_ref[...], gkv_ref[...]).astype(_MXU_DTYPE)
    kv = _mm(kvn, wkv_ref[...])
    k_rope = _rotate(kr_ref[...], cos_t, sin_t, half, MLA_ROPE).astype(k_ref.dtype)
    lane = lax.broadcasted_iota(jnp.int32, (q.shape[0], LANES), 1)
    H = MLA_HEADS
    for pair in range(H // 2):
        c0 = H * MLA_NOPE + pair * LANES
        r = _rotate(q[:, c0:c0 + LANES], cos_t, sin_t, half, MLA_ROPE)
        r_even = jnp.where(lane < MLA_ROPE, r, 0.0)
        r_odd = jnp.where(lane < MLA_ROPE, pltpu.roll(r, MLA_ROPE, 1), 0.0)
        for h, rr in ((2 * pair, r_even), (2 * pair + 1, r_odd)):
            q_ref[:, h * MLA_QK:h * MLA_QK + LANES] = q[:, h * MLA_NOPE:(h + 1) * MLA_NOPE].astype(q_ref.dtype)
            q_ref[:, h * MLA_QK + LANES:(h + 1) * MLA_QK] = rr.astype(q_ref.dtype)
    for h in range(H):
        k_ref[:, h * MLA_QK:h * MLA_QK + LANES] = kv[:, h * MLA_NOPE:(h + 1) * MLA_NOPE].astype(k_ref.dtype)
        k_ref[:, h * MLA_QK + LANES:(h + 1) * MLA_QK] = k_rope
    v_ref[...] = kv[:, H * MLA_NOPE:].astype(v_ref.dtype)


def _mla_up(p, pos, invf, sign, gq, gkv, wq, wkv, S):
    N = p.shape[0]
    tm = _pick(S, 512, 8)
    H = MLA_HEADS
    kern = functools.partial(_mla_up_kernel, scale=(MLA_NOPE + MLA_ROPE) ** -0.5)
    row = lambda i: (0, 0)
    return pl.pallas_call(
        kern,
        grid=(N // tm,),
        in_specs=[
            pl.BlockSpec((tm, MLA_Q_LORA), lambda i: (i, COL_CQ // MLA_Q_LORA)),
            pl.BlockSpec((tm, MLA_KV_LORA), lambda i: (i, COL_CKV // MLA_KV_LORA)),
            pl.BlockSpec((tm, LANES), lambda i: (i, COL_KR // LANES)),
            pl.BlockSpec((tm, 1), lambda i: (i, 0)),
            pl.BlockSpec((1, LANES), row),
            pl.BlockSpec((1, LANES), row),
            pl.BlockSpec((1, MLA_Q_LORA), row),
            pl.BlockSpec((1, MLA_KV_LORA), row),
            pl.BlockSpec(wq.shape, row),
            pl.BlockSpec(wkv.shape, row),
        ],
        out_specs=[
            pl.BlockSpec((tm, H * MLA_QK), lambda i: (i, 0)),
            pl.BlockSpec((tm, H * MLA_QK), lambda i: (i, 0)),
            pl.BlockSpec((tm, H * MLA_V), lambda i: (i, 0)),
        ],
        out_shape=[
            jax.ShapeDtypeStruct((N, H * MLA_QK), _MXU_DTYPE),
            jax.ShapeDtypeStruct((N, H * MLA_QK), _MXU_DTYPE),
            jax.ShapeDtypeStruct((N, H * MLA_V), _MXU_DTYPE),
        ],
        compiler_params=_cparams(("arbitrary",)),
        name="mla_up",
    )(p, p, p, pos, invf, sign, gq, gkv, wq, wkv)


def _flash_kernel(*refs, tq, tk, scale, decay):
    if decay:
        q_ref, k_ref, v_ref, cq_ref, ck_ref, o_ref = refs
    else:
        q_ref, k_ref, v_ref, o_ref = refs
    i = pl.program_id(2)
    q = q_ref[...]
    if scale != 1.0:
        q = q.astype(f32) * scale
    q = q.astype(_MXU_DTYPE)
    row = i * tq + lax.broadcasted_iota(jnp.int32, (tq, 1), 0)
    if decay:
        cq = cq_ref[0, 0]

    def body(kt, carry):
        m, l, acc = carry
        k0 = pl.multiple_of(kt * tk, tk)
        k = k_ref[pl.ds(k0, tk), :].astype(_MXU_DTYPE)
        v = v_ref[pl.ds(k0, tk), :].astype(_MXU_DTYPE)
        s = _mm_nt(q, k)
        if decay:
            s = (s - ck_ref[0, 0, :, pl.ds(k0, tk)]) + cq
        col = k0 + lax.broadcasted_iota(jnp.int32, (1, tk), 1)
        s = jnp.where(col <= row, s, NEG)
        m_new = jnp.maximum(m, jnp.max(s, axis=1, keepdims=True))
        p = jnp.exp(s - m_new)
        alpha = jnp.exp(m - m_new)
        l = alpha * l + jnp.sum(p, axis=1, keepdims=True)
        acc = alpha * acc + _mm(p.astype(_MXU_DTYPE), v)
        return m_new, l, acc

    dv = v_ref.shape[1]
    n_kt = ((i + 1) * tq + tk - 1) // tk
    init = (jnp.full((tq, 1), NEG, f32), jnp.zeros((tq, 1), f32), jnp.zeros((tq, dv), f32))
    m, l, acc = lax.fori_loop(0, n_kt, body, init)
    o_ref[...] = (acc / l).astype(o_ref.dtype)


def _flash(q, k, v, *, B, S, H, dk, dv, qcol, kcol, vcol, scale, cum_col=None, cum_row=None):
    N = B * S
    tq = _pick(S, 512, 8)
    tk = _pick(S, 256, 8)
    nq = S // tq
    decay = cum_col is not None
    in_specs = [
        pl.BlockSpec((tq, dk), lambda b, h, i: (b * nq + i, qcol + h)),
        pl.BlockSpec((S, dk), lambda b, h, i: (b, kcol + h)),
        pl.BlockSpec((S, dv), lambda b, h, i: (b, vcol + h)),
    ]
    args = [q, k, v]
    if decay:
        in_specs += [
            pl.BlockSpec((1, 1, tq, 1), lambda b, h, i: (b, h, i, 0)),
            pl.BlockSpec((1, 1, 1, S), lambda b, h, i: (b, h, 0, 0)),
        ]
        args += [cum_col, cum_row]
    kern = functools.partial(_flash_kernel, tq=tq, tk=tk, scale=scale, decay=decay)
    return pl.pallas_call(
        kern,
        grid=(B, H, nq),
        in_specs=in_specs,
        out_specs=pl.BlockSpec((tq, dv), lambda b, h, i: (b * nq + i, h)),
        out_shape=jax.ShapeDtypeStruct((N, H * dv), f32),
        compiler_params=_cparams(("arbitrary", "arbitrary", "arbitrary")),
        name="flash_fox" if decay else "flash_mla",
    )(*args)


def _fox_prep_kernel(ff_ref, b_ref, o_ref, *, chunk):
    S = ff_ref.shape[0]
    r = lax.broadcasted_iota(jnp.int32, (chunk, chunk), 0)
    c = lax.broadcasted_iota(jnp.int32, (chunk, chunk), 1)
    tri = (c <= r).astype(_MXU_DTYPE)

    def body(j, carry):
        r0 = pl.multiple_of(j * chunk, chunk)
        x = ff_ref[pl.ds(r0, chunk), :] + b_ref[...]
        lf = jnp.minimum(x, 0.0) - jnp.log(1.0 + jnp.exp(-jnp.abs(x)))
        hi, mid, lo = _split3(lf)
        cum = (_mm(tri, hi) + _mm(tri, mid)) + _mm(tri, lo) + carry
        o_ref[pl.ds(r0, chunk), :] = cum
        return cum[chunk - 1:chunk, :]

    lax.fori_loop(0, S // chunk, body, jnp.zeros((1, LANES), f32))


def _fox_prep(p, b_pad, B, S):
    N = B * S
    kern = functools.partial(_fox_prep_kernel, chunk=_pick(S, 128, 8))
    return pl.pallas_call(
        kern,
        grid=(B,),
        in_specs=[
            pl.BlockSpec((S, LANES), lambda b: (b, COL_FF // LANES)),
            pl.BlockSpec((1, LANES), lambda b: (0, 0)),
        ],
        out_specs=pl.BlockSpec((S, LANES), lambda b: (b, 0)),
        out_shape=jax.ShapeDtypeStruct((N, LANES), f32),
        compiler_params=_cparams(("arbitrary",)),
        name="fox_prep",
    )(p, b_pad)


def _nsa_prep_kernel(q_ref, kc_ref, vc_ref, ks_ref, vs_ref, kw_ref, vw_ref, pos_ref, invf_ref, sign_ref,
                     qo_ref, cmp_ref, kso_ref, vso_ref, kwo_ref, vwo_ref, *, scale):
    cos_t, sin_t = _rot_tables(pos_ref, invf_ref, sign_ref)
    half = PARTIAL_ROT // 2
    rot = lambda x: _rotate(x, cos_t, sin_t, half, LANES)
    for h in range(NSA_HEADS):
        sl = slice(h * HEAD_DIM, (h + 1) * HEAD_DIM)
        qo_ref[:, sl] = (rot(q_ref[:, sl]) * scale).astype(qo_ref.dtype)
    for g in range(NSA_KV_GROUPS):
        sl = slice(g * HEAD_DIM, (g + 1) * HEAD_DIM)
        cmp_ref[0, g] = rot(kc_ref[:, sl]).astype(cmp_ref.dtype)
        cmp_ref[1, g] = vc_ref[:, sl].astype(cmp_ref.dtype)
        kso_ref[:, sl] = rot(ks_ref[:, sl]).astype(kso_ref.dtype)
        kwo_ref[:, sl] = rot(kw_ref[:, sl]).astype(kwo_ref.dtype)
    vso_ref[...] = vs_ref[...].astype(vso_ref.dtype)
    vwo_ref[...] = vw_ref[...].astype(vwo_ref.dtype)


def _nsa_prep(p, pos, invf, sign, S):
    N = p.shape[0]
    tm = _pick(S, 512, 8)
    G = NSA_KV_GROUPS
    kv = lambda col: pl.BlockSpec((tm, NSA_KV), lambda i: (i, col // NSA_KV))
    row = lambda i: (0, 0)
    kern = functools.partial(_nsa_prep_kernel, scale=HEAD_DIM ** -0.5)
    return pl.pallas_call(
        kern,
        grid=(N // tm,),
        in_specs=[
            pl.BlockSpec((tm, W_NSA), lambda i: (i, COL_NQ // W_NSA)),
            kv(COL_NKC), kv(COL_NVC), kv(COL_NKS), kv(COL_NVS), kv(COL_NKW), kv(COL_NVW),
            pl.BlockSpec((tm, 1), lambda i: (i, 0)),
            pl.BlockSpec((1, LANES), row),
            pl.BlockSpec((1, LANES), row),
        ],
        out_specs=[
            pl.BlockSpec((tm, W_NSA), lambda i: (i, 0)),
            pl.BlockSpec((2, G, tm, HEAD_DIM), lambda i: (0, 0, i, 0)),
            pl.BlockSpec((tm, NSA_KV), lambda i: (i, 0)),
            pl.BlockSpec((tm, NSA_KV), lambda i: (i, 0)),
            pl.BlockSpec((tm, NSA_KV), lambda i: (i, 0)),
            pl.BlockSpec((tm, NSA_KV), lambda i: (i, 0)),
        ],
        out_shape=[
            jax.ShapeDtypeStruct((N, W_NSA), _MXU_DTYPE),
            jax.ShapeDtypeStruct((2, G, N, HEAD_DIM), _MXU_DTYPE),
            jax.ShapeDtypeStruct((N, NSA_KV), _MXU_DTYPE),
            jax.ShapeDtypeStruct((N, NSA_KV), _MXU_DTYPE),
            jax.ShapeDtypeStruct((N, NSA_KV), _MXU_DTYPE),
            jax.ShapeDtypeStruct((N, NSA_KV), _MXU_DTYPE),
        ],
        compiler_params=_cparams(("arbitrary",)),
        name="nsa_prep",
    )(p, p, p, p, p, p, p, pos, invf, sign)


def _compress_kernel(c_ref, pos_ref, w1_ref, w2_ref, o_ref):
    x = c_ref[0, 0, 0]
    half = x.shape[1]
    nrow = x.shape[0]
    a = _mm(x, w1_ref[0, :half, :])
    b = _mm(x, w1_ref[0, half:, :])
    posb = _mm(pos_ref[0].astype(_MXU_DTYPE), w1_ref[0])[0:1, :]
    hid = a + pltpu.roll(b, nrow - 1, 0) + posb
    c0 = math.sqrt(2.0 / math.pi)
    act = 0.5 * hid * (1.0 + jnp.tanh(c0 * (hid + 0.044715 * (hid * hid * hid))))
    o_ref[0, 0, 0] = _mm(act.astype(_MXU_DTYPE), w2_ref[0])


def _compress(cmp_in, pos_flat, w1, w2, B, S):
    G = NSA_KV_GROUPS
    nch = S // NSA_CMP_STRIDE
    width = NSA_CMP_STRIDE * HEAD_DIM
    x = cmp_in.reshape(2, G, B, nch, width)
    return pl.pallas_call(
        _compress_kernel,
        grid=(2, G, B),
        in_specs=[
            pl.BlockSpec((1, 1, 1, nch, width), lambda t, g, b: (t, g, b, 0, 0)),
            pl.BlockSpec((1, 8, 2 * width), lambda t, g, b: (t, 0, 0)),
            pl.BlockSpec((1, 2 * width, NSA_CMP_HIDDEN), lambda t, g, b: (t, 0, 0)),
            pl.BlockSpec((1, NSA_CMP_HIDDEN, HEAD_DIM), lambda t, g, b: (t, 0, 0)),
        ],
        out_specs=pl.BlockSpec((1, 1, 1, nch, HEAD_DIM), lambda t, g, b: (t, g, b, 0, 0)),
        out_shape=jax.ShapeDtypeStruct((2, G, B, nch, HEAD_DIM), f32),
        compiler_params=_cparams(("arbitrary", "arbitrary", "arbitrary")),
        name="nsa_compress",
    )(x, pos_flat, w1, w2)


def _nsa_attn_kernel(q_ref, kc_ref, vc_ref, ks_ref, vs_ref, kw_ref, vw_ref, g_ref, ovl_ref, exp_ref, o_ref,
                     *, tq, tk, tkw, n_slc, top_k):
    i = pl.program_id(2)
    Hg = NSA_GQA
    D = HEAD_DIM
    q4 = q_ref[...]
    qs = jnp.concatenate([q4[:, h * D:(h + 1) * D] for h in range(Hg)], axis=0)
    t = i * tq + lax.broadcasted_iota(jnp.int32, (tq, 1), 0)
    stack = lambda a: jnp.concatenate([a] * Hg, axis=0)

    kc = kc_ref[0, 0, 0].astype(_MXU_DTYPE)
    vc = vc_ref[0, 0, 0].astype(_MXU_DTYPE)
    ncmp = kc.shape[0]
    cend = lax.broadcasted_iota(jnp.int32, (1, ncmp), 1) * NSA_CMP_STRIDE + (NSA_CMP_LEN - 1)
    valid = stack(cend <= t)
    s = jnp.where(valid, _mm_nt(qs, kc), NEG)
    e = jnp.where(valid, jnp.exp(s - jnp.max(s, axis=1, keepdims=True)), 0.0)
    l = jnp.sum(e, axis=1, keepdims=True)
    p = e / jnp.where(l > 0.0, l, 1.0)
    o_cmp = _mm(p.astype(_MXU_DTYPE), vc)

    psum = p[0:tq]
    for h in range(1, Hg):
        psum = psum + p[h * tq:(h + 1) * tq]
    hi, mid, lo = _split3(psum)
    ovl = ovl_ref[...]
    imp = (_mm(hi, ovl) + _mm(mid, ovl)) + _mm(lo, ovl)
    j = lax.broadcasted_iota(jnp.int32, (1, LANES), 1)
    cur = t // NSA_SLC_BLOCK
    forced = (j == 0) | (j == cur) | (j == cur - 1)
    imp = jnp.where(forced, BIG, jnp.where(j > cur, NEG, imp))
    rank = jnp.zeros((tq, LANES), f32)
    for kk in range(n_slc):
        ck = imp[:, kk:kk + 1]
        before = (ck > imp) | ((ck == imp) & (j > kk))
        rank = rank + before.astype(f32)
    sel = ((rank < float(top_k)) & (j <= cur)).astype(_MXU_DTYPE)

    def online(s, mask, v, carry):
        m, l, acc = carry
        m_new = jnp.maximum(m, jnp.max(s, axis=1, keepdims=True))
        pe = jnp.where(mask, jnp.exp(s - m_new), 0.0)
        alpha = jnp.exp(m - m_new)
        return m_new, alpha * l + jnp.sum(pe, axis=1, keepdims=True), alpha * acc + _mm(pe.astype(_MXU_DTYPE), v)

    init = (jnp.full((Hg * tq, 1), NEG, f32), jnp.zeros((Hg * tq, 1), f32), jnp.zeros((Hg * tq, D), f32))

    def slc_body(kt, carry):
        k0 = pl.multiple_of(kt * tk, tk)
        chosen = _mm(sel, exp_ref[:, pl.ds(k0, tk)])
        col = k0 + lax.broadcasted_iota(jnp.int32, (1, tk), 1)
        mask = stack((chosen > 0.5) & (col <= t))
        s = jnp.where(mask, _mm_nt(qs, ks_ref[pl.ds(k0, tk), :]), NEG)
        return online(s, mask, vs_ref[pl.ds(k0, tk), :], carry)

    _, l_s, acc_s = lax.fori_loop(0, ((i + 1) * tq + tk - 1) // tk, slc_body, init)
    o_slc = acc_s / l_s

    def win_body(kt, carry):
        k0 = pl.multiple_of(kt * tkw, tkw)
        col = k0 + lax.broadcasted_iota(jnp.int32, (1, tkw), 1)
        mask = stack((col <= t) & (t - col < NSA_WINDOW))
        s = jnp.where(mask, _mm_nt(qs, kw_ref[pl.ds(k0, tkw), :]), NEG)
        return online(s, mask, vw_ref[pl.ds(k0, tkw), :], carry)

    lo_kt = jnp.maximum(i * tq - (NSA_WINDOW - 1), 0) // tkw
    _, l_w, acc_w = lax.fori_loop(lo_kt, ((i + 1) * tq + tkw - 1) // tkw, win_body, init)
    o_win = acc_w / l_w

    gate = jax.nn.sigmoid(g_ref[...])
    for h in range(Hg):
        rows = slice(h * tq, (h + 1) * tq)
        o = (gate[:, 3 * h:3 * h + 1] * o_cmp[rows] + gate[:, 3 * h + 1:3 * h + 2] * o_slc[rows]
             + gate[:, 3 * h + 2:3 * h + 3] * o_win[rows])
        o_ref[:, h * D:(h + 1) * D] = o.astype(o_ref.dtype)


def _nsa_attn(qn, cmp_kv, ks, vs, kw, vw, p, ovl, expand, B, S):
    N = B * S
    G = NSA_KV_GROUPS
    D = HEAD_DIM
    tq = _pick(S, 128, 8)
    tk = _pick(S, 256, 8)
    tkw = _pick(S, 128, 8)
    nq = S // tq
    nch = cmp_kv.shape[3]
    n_slc = S // NSA_SLC_BLOCK
    assert n_slc <= LANES
    kern = functools.partial(_nsa_attn_kernel, tq=tq, tk=tk, tkw=tkw, n_slc=n_slc, top_k=min(NSA_TOPK, n_slc))
    seq = lambda: pl.BlockSpec((S, D), lambda b, g, i: (b, g))
    return pl.pallas_call(
        kern,
        grid=(B, G, nq),
        in_specs=[
            pl.BlockSpec((tq, NSA_GQA * D), lambda b, g, i: (b * nq + i, g)),
            pl.BlockSpec((1, 1, 1, nch, D), lambda b, g, i: (0, g, b, 0, 0)),
            pl.BlockSpec((1, 1, 1, nch, D), lambda b, g, i: (1, g, b, 0, 0)),
            seq(), seq(), seq(), seq(),
            pl.BlockSpec((tq, LANES), lambda b, g, i: (b * nq + i, COL_NG // LANES + g)),
            pl.BlockSpec(ovl.shape, lambda b, g, i: (0, 0)),
            pl.BlockSpec(expand.shape, lambda b, g, i: (0, 0)),
        ],
        out_specs=pl.BlockSpec((tq, NSA_GQA * D), lambda b, g, i: (b * nq + i, g)),
        out_shape=jax.ShapeDtypeStruct((N, W_NSA), f32),
        compiler_params=_cparams(("arbitrary", "arbitrary", "arbitrary")),
        name="nsa_attn",
    )(qn, cmp_kv, cmp_kv, ks, vs, kw, vw, p, ovl, expand)


def _nsa_consts(S):
    nch = S // NSA_CMP_STRIDE
    n_slc = S // NSA_SLC_BLOCK
    c_start = np.arange(nch) * NSA_CMP_STRIDE
    s_start = np.arange(n_slc) * NSA_SLC_BLOCK
    ov = (np.minimum(c_start[:, None] + NSA_CMP_LEN, s_start[None, :] + NSA_SLC_BLOCK)
          - np.maximum(c_start[:, None], s_start[None, :]))
    ovl = np.zeros((nch, LANES), np.float32)
    ovl[:, :n_slc] = np.clip(ov, 0, None) / NSA_CMP_LEN
    expand = np.zeros((LANES, S), np.float32)
    expand[np.arange(S) // NSA_SLC_BLOCK, np.arange(S)] = 1.0
    return jnp.asarray(ovl, _MXU_DTYPE), jnp.asarray(expand, _MXU_DTYPE)


def _out_proj_kernel(om_ref, of_ref, on_ref, gm_ref, gf_ref, gn_ref, w_ref, x_ref, gt_ref, o_ref, h_ref):
    @pl.when(pl.program_id(1) == 0)
    def _():
        h_ref[:, :W_MLA] = _rms(om_ref[...], gm_ref[...]).astype(h_ref.dtype)
        h_ref[:, W_MLA:W_MLA + W_FOX] = _rms(of_ref[...], gf_ref[...]).astype(h_ref.dtype)
        h_ref[:, W_MLA + W_FOX:] = _rms(on_ref[...], gn_ref[...]).astype(h_ref.dtype)

    o_ref[...] = x_ref[...] + gt_ref[0] * _mm(h_ref[...], w_ref[...])


def _out_proj(o_mla, o_fox, o_nsa, gm, gf, gn, w, x2, mod3, S):
    N, D = x2.shape
    DM = w.shape[0]
    tm = _pick(S, 512, 8)
    tn = _pick(D, 1024)
    nj = D // tn
    row = lambda i, j: (0, 0)
    return pl.pallas_call(
        _out_proj_kernel,
        grid=(N // tm, nj),
        in_specs=[
            pl.BlockSpec((tm, W_MLA), lambda i, j: (i, 0)),
            pl.BlockSpec((tm, W_FOX), lambda i, j: (i, 0)),
            pl.BlockSpec((tm, W_NSA), lambda i, j: (i, 0)),
            pl.BlockSpec((1, W_MLA), row),
            pl.BlockSpec((1, W_FOX), row),
            pl.BlockSpec((1, W_NSA), row),
            pl.BlockSpec((DM, tn), lambda i, j: (0, j)),
            pl.BlockSpec((tm, tn), lambda i, j: (i, j)),
            pl.BlockSpec((1, 1, tn), lambda i, j: (i * tm // S, 0, 2 * nj + j)),
        ],
        out_specs=pl.BlockSpec((tm, tn), lambda i, j: (i, j)),
        out_shape=jax.ShapeDtypeStruct((N, D), f32),
        scratch_shapes=[pltpu.VMEM((tm, DM), _MXU_DTYPE)],
        compiler_params=_cparams(("arbitrary", "arbitrary")),
        name="out_proj",
    )(o_mla, o_fox, o_nsa, gm, gf, gn, w, x2, mod3)


def _mlp_kernel(x_ref, sh_ref, sc_ref, gt_ref, g_ref, w1_ref, w2_ref, fg_ref, o_ref, h_ref, acc_ref, *, final):
    f = pl.program_id(1)

    @pl.when(f == 0)
    def _():
        h = _rms(x_ref[...], g_ref[...]) * (1.0 + sc_ref[0]) + sh_ref[0]
        h_ref[...] = h.astype(h_ref.dtype)
        acc_ref[...] = jnp.zeros_like(acc_ref)

    u = jnp.maximum(_mm(h_ref[...], w1_ref[...]), 0.0)
    acc_ref[...] += _mm((u * u).astype(_MXU_DTYPE), w2_ref[...])

    @pl.when(f == pl.num_programs(1) - 1)
    def _():
        y = x_ref[...] + gt_ref[0] * acc_ref[...]
        if final:
            y = _rms(y, fg_ref[...])
        o_ref[...] = y


def _mlp(x2, mod3, gain, w1, w2, final_gain, S, final):
    N, D = x2.shape
    F = w1.shape[1]
    tm = _pick(S, 512, 8)
    tf = _pick(F, 512)
    row = lambda i, f: (0, 0)
    mod_blk = lambda c: pl.BlockSpec((1, 1, D), lambda i, f: (i * tm // S, 0, c))
    return pl.pallas_call(
        functools.partial(_mlp_kernel, final=final),
        grid=(N // tm, F // tf),
        in_specs=[
            pl.BlockSpec((tm, D), lambda i, f: (i, 0)),
            mod_blk(3), mod_blk(4), mod_blk(5),
            pl.BlockSpec((1, D), row),
            pl.BlockSpec((D, tf), lambda i, f: (0, f)),
            pl.BlockSpec((tf, D), lambda i, f: (f, 0)),
            pl.BlockSpec((1, D), row),
        ],
        out_specs=pl.BlockSpec((tm, D), lambda i, f: (i, 0)),
        out_shape=jax.ShapeDtypeStruct((N, D), f32),
        scratch_shapes=[pltpu.VMEM((tm, D), _MXU_DTYPE), pltpu.VMEM((tm, D), f32)],
        compiler_params=_cparams(("arbitrary", "arbitrary")),
        name="mlp",
    )(x2, mod3, mod3, mod3, gain, w1, w2, final_gain)


def _prep_w_in(w):
    D = w.shape[0]
    o = np.cumsum((0, MLA_Q_LORA, MLA_KV_LORA, MLA_ROPE, W_FOX, W_FOX, W_FOX, FOX_HEADS,
                   W_NSA, NSA_KV, NSA_KV, NSA_KV, NSA_KV, NSA_KV, NSA_KV, 3 * NSA_HEADS))
    (cq, ckv, kr, fq, fk, fv, ff, nq, nkc, nvc, nks, nvs, nkw, nvw, ng) = [w[:, a:b] for a, b in zip(o[:-1], o[1:])]
    z = lambda n: jnp.zeros((D, n), w.dtype)
    gw = 3 * NSA_GQA
    cols = [nq, cq, fq, fk, fv, ckv, nkc, nvc, nks, nvs, nkw, nvw,
            kr, z(LANES - MLA_ROPE), ff, z(LANES - FOX_HEADS),
            ng[:, :gw], z(LANES - gw), ng[:, gw:], z(LANES - gw)]
    out = jnp.concatenate(cols, axis=1).astype(_MXU_DTYPE)
    assert out.shape[1] == N_COLS
    return out


def _prep_mla_w(w_uq, w_ukv):
    H = MLA_HEADS
    qd = MLA_NOPE + MLA_ROPE
    q_nope = np.concatenate([np.arange(h * qd, h * qd + MLA_NOPE) for h in range(H)])
    q_rope = np.concatenate([np.arange(h * qd + MLA_NOPE, (h + 1) * qd) for h in range(H)])
    kd = MLA_NOPE + MLA_V
    k_nope = np.concatenate([np.arange(h * kd, h * kd + MLA_NOPE) for h in range(H)])
    v_cols = np.concatenate([np.arange(h * kd + MLA_NOPE, (h + 1) * kd) for h in range(H)])
    wq = jnp.take(w_uq, jnp.asarray(np.concatenate([q_nope, q_rope])), axis=1).astype(_MXU_DTYPE)
    wkv = jnp.take(w_ukv, jnp.asarray(np.concatenate([k_nope, v_cols])), axis=1).astype(_MXU_DTYPE)
    return wq, wkv


def _pad_lanes(v):
    return jnp.zeros((1, LANES), f32).at[0, :v.shape[0]].set(v.astype(f32))


def kernel(x, c, positions, ada_w, ada_b, norm_mix, norm_mlp, w_in, fox_b_f, mla_q_norm, mla_w_uq, mla_kv_norm,
           mla_w_ukv, nsa_pos_k, nsa_pos_v, nsa_cmp_k_w1, nsa_cmp_k_w2, nsa_cmp_v_w1, nsa_cmp_v_w2, out_norm,
           w_out, mlp_w1, mlp_w2, final_norm):
    B, S, D = x.shape
    L = ada_w.shape[0]
    N = B * S
    assert S % LANES == 0 and D % LANES == 0 and B <= 8

    x2 = x.reshape(N, D)
    pos = positions.reshape(N, 1).astype(jnp.int32)
    c_pad = jnp.zeros((8, D), f32).at[:B].set(c)
    mod = _ada_mod(c_pad, ada_w, ada_b)

    invf_mla, sign_mla = _rope_consts(MLA_ROPE // 2, MLA_ROPE, LANES)
    invf_nsa, sign_nsa = _rope_consts(PARTIAL_ROT // 2, LANES, LANES)
    ovl, expand = _nsa_consts(S)
    row = lambda v: v.reshape(1, -1)

    for l in range(L):
        mod3 = mod[l, :B].reshape(B, 1, 6 * D)
        p = _proj_in(x2, mod3, row(norm_mix[l]), _prep_w_in(w_in[l]), S)

        wq, wkv = _prep_mla_w(mla_w_uq[l], mla_w_ukv[l])
        q_m, k_m, v_m = _mla_up(p, pos, invf_mla, sign_mla, row(mla_q_norm[l]), row(mla_kv_norm[l]), wq, wkv, S)
        o_mla = _flash(q_m, k_m, v_m, B=B, S=S, H=MLA_HEADS, dk=MLA_QK, dv=MLA_V, qcol=0, kcol=0, vcol=0,
                       scale=1.0)

        cum = _fox_prep(p, _pad_lanes(fox_b_f[l]), B, S)
        cum_h = cum[:, :FOX_HEADS].reshape(B, S, FOX_HEADS).transpose(0, 2, 1)
        o_fox = _flash(p, p, p, B=B, S=S, H=FOX_HEADS, dk=HEAD_DIM, dv=HEAD_DIM,
                       qcol=COL_FQ // HEAD_DIM, kcol=COL_FK // HEAD_DIM, vcol=COL_FV // HEAD_DIM,
                       scale=HEAD_DIM ** -0.5, cum_col=cum_h[..., None], cum_row=cum_h[:, :, None, :])

        qn, cmp_in, ks, vs, kw, vw = _nsa_prep(p, pos, invf_nsa, sign_nsa, S)
        flat = lambda a: jnp.broadcast_to(a.reshape(1, -1), (8, a.size))
        cmp_kv = _compress(
            cmp_in,
            jnp.stack([flat(nsa_pos_k[l]), flat(nsa_pos_v[l])]).astype(f32),
            jnp.stack([nsa_cmp_k_w1[l], nsa_cmp_v_w1[l]]).astype(_MXU_DTYPE),
            jnp.stack([nsa_cmp_k_w2[l], nsa_cmp_v_w2[l]]).astype(_MXU_DTYPE), B, S)
        o_nsa = _nsa_attn(qn, cmp_kv, ks, vs, kw, vw, p, ovl, expand, B, S)

        gn = out_norm[l]
        x2 = _out_proj(o_mla, o_fox, o_nsa, row(gn[:W_MLA]), row(gn[W_MLA:W_MLA + W_FOX]),
                       row(gn[W_MLA + W_FOX:]), w_out[l].astype(_MXU_DTYPE), x2, mod3, S)
        x2 = _mlp(x2, mod3, row(norm_mlp[l]), mlp_w1[l].astype(_MXU_DTYPE), mlp_w2[l].astype(_MXU_DTYPE),
                  row(final_norm), S, final=(l == L - 1))
    return x2.reshape(B, S, D)
```

```python
import functools
import math

import numpy as np
import jax
import jax.numpy as jnp
from jax import lax
from jax.experimental import pallas as pl
from jax.experimental.pallas import tpu as pltpu

f32 = jnp.float32
_MXU_DTYPE = jnp.bfloat16

HEAD_DIM = 128
MLA_HEADS = 4
MLA_Q_LORA = 512
MLA_KV_LORA = 256
MLA_NOPE = 128
MLA_ROPE = 64
MLA_V = 128
FOX_HEADS = 4
NSA_HEADS = 8
NSA_KV_GROUPS = 2
NSA_GQA = NSA_HEADS // NSA_KV_GROUPS
NSA_CMP_LEN = 32
NSA_CMP_STRIDE = 16
NSA_CMP_HIDDEN = 512
NSA_SLC_BLOCK = 64
NSA_TOPK = 16
NSA_WINDOW = 512
W_MLA = MLA_HEADS * MLA_V
W_FOX = FOX_HEADS * HEAD_DIM
W_NSA = NSA_HEADS * HEAD_DIM
NSA_KV = NSA_KV_GROUPS * HEAD_DIM
ROPE_THETA = 500000.0
PARTIAL_ROT = HEAD_DIM // 4
EPS = 1e-6
NEG = -1e30
BIG = 1e30

LANES = 128
VMEM_LIMIT = 56 * 1024 * 1024

MLA_QK = 2 * LANES
COL_NQ = 0
COL_CQ = 1024
COL_FQ = 1536
COL_FK = 2048
COL_FV = 2560
COL_CKV = 3072
COL_NKC = 3328
COL_NVC = 3584
COL_NKS = 3840
COL_NVS = 4096
COL_NKW = 4352
COL_NVW = 4608
COL_KR = 4864
COL_FF = 4992
COL_NG = 5120
N_COLS = 5376


def _pick(n, cap, mult=LANES):
    if n <= cap:
        return n
    t = (cap // mult) * mult
    while t >= mult:
        if n % t == 0:
            return t
        t -= mult
    raise ValueError(f"no tile for {n} under {cap}")


def _mm(a, b):
    return jnp.dot(a, b, preferred_element_type=f32)


def _mm_nt(a, b):
    return lax.dot_general(a, b, (((1,), (1,)), ((), ())), preferred_element_type=f32)


def _split3(a):
    hi = a.astype(_MXU_DTYPE)
    r1 = a - hi.astype(f32)
    mid = r1.astype(_MXU_DTYPE)
    lo = (r1 - mid.astype(f32)).astype(_MXU_DTYPE)
    return hi, mid, lo


def _rms(x, gain):
    return x * lax.rsqrt(jnp.mean(x * x, axis=-1, keepdims=True) + EPS) * gain


def _cparams(sem):
    return pltpu.CompilerParams(dimension_semantics=sem, vmem_limit_bytes=VMEM_LIMIT)


def _ada_kernel(c_ref, w_ref, b_ref, o_ref):
    c = c_ref[...]
    ca = c * jax.nn.sigmoid(c)
    w = w_ref[0]
    c_hi, c_mid, _ = _split3(ca)
    w_hi, w_mid, _ = _split3(w)
    acc = _mm(c_hi, w_hi) + (_mm(c_mid, w_hi) + _mm(c_hi, w_mid))
    o_ref[0] = acc + b_ref[0]


def _ada_mod(c_pad, ada_w, ada_b):
    L, D, D6 = ada_w.shape
    tn = _pick(D6, 1024)
    return pl.pallas_call(
        _ada_kernel,
        grid=(L, D6 // tn),
        in_specs=[
            pl.BlockSpec((8, D), lambda l, j: (0, 0)),
            pl.BlockSpec((1, D, tn), lambda l, j: (l, 0, j)),
            pl.BlockSpec((1, 1, tn), lambda l, j: (l, 0, j)),
        ],
        out_specs=pl.BlockSpec((1, 8, tn), lambda l, j: (l, 0, j)),
        out_shape=jax.ShapeDtypeStruct((L, 8, D6), f32),
        compiler_params=_cparams(("arbitrary", "arbitrary")),
        name="ada_mod",
    )(c_pad, ada_w, ada_b.reshape(L, 1, D6))


def _proj_in_kernel(x_ref, sh_ref, sc_ref, g_ref, w_ref, o_ref, h_ref):
    @pl.when(pl.program_id(1) == 0)
    def _():
        h = _rms(x_ref[...], g_ref[...]) * (1.0 + sc_ref[0]) + sh_ref[0]
        h_ref[...] = h.astype(h_ref.dtype)

    o_ref[...] = _mm(h_ref[...], w_ref[...])


def _proj_in(x2, mod3, gain, w, S):
    N, D = x2.shape
    NC = w.shape[1]
    tm = _pick(S, 512, 8)
    tn = _pick(NC, 768)
    return pl.pallas_call(
        _proj_in_kernel,
        grid=(N // tm, NC // tn),
        in_specs=[
            pl.BlockSpec((tm, D), lambda i, j: (i, 0)),
            pl.BlockSpec((1, 1, D), lambda i, j: (i * tm // S, 0, 0)),
            pl.BlockSpec((1, 1, D), lambda i, j: (i * tm // S, 0, 1)),
            pl.BlockSpec((1, D), lambda i, j: (0, 0)),
            pl.BlockSpec((D, tn), lambda i, j: (0, j)),
        ],
        out_specs=pl.BlockSpec((tm, tn), lambda i, j: (i, j)),
        out_shape=jax.ShapeDtypeStruct((N, NC), f32),
        scratch_shapes=[pltpu.VMEM((tm, D), _MXU_DTYPE)],
        compiler_params=_cparams(("arbitrary", "arbitrary")),
        name="proj_in",
    )(x2, mod3, mod3, gain, w)


def _rot_tables(pos_ref, invf_ref, sign_ref):
    ang = pos_ref[...].astype(f32) * invf_ref[...]
    return jnp.cos(ang), jnp.sin(ang) * sign_ref[...]


def _rotate(x, cos_t, sin_t, half, period):
    lane = lax.broadcasted_iota(jnp.int32, x.shape, 1)
    partner = jnp.where((lane % period) < half,
                        pltpu.roll(x, LANES - half, 1),
                        pltpu.roll(x, half, 1))
    return x * cos_t + partner * sin_t


def _rope_consts(half, period, width):
    inv = ROPE_THETA ** (-jnp.arange(half, dtype=f32) / half)
    lane = np.arange(LANES)
    in_rot = ((lane % period) < 2 * half) & (lane < width)
    idx = (lane % period) % half
    invf = jnp.where(jnp.asarray(in_rot), inv[idx], 0.0).reshape(1, LANES).astype(f32)
    sign = np.where(in_rot, np.where((lane % period) < half, -1.0, 1.0), 0.0)
    return invf, jnp.asarray(sign, f32).reshape(1, LANES)


def _mla_up_kernel(cq_ref, ckv_ref, kr_ref, pos_ref, invf_ref, sign_ref, gq_ref, gkv_ref, wq_ref, wkv_ref,
                   q_ref, k_ref, v_ref, *, scale):
    cos_t, sin_t = _rot_tables(pos_ref, invf_ref, sign_ref)
    half = MLA_ROPE // 2
    qn = _rms(cq_ref[...], gq_ref[...]).astype(_MXU_DTYPE)
    q = _mm(qn, wq_ref[...]) * scale
    kvn = _rms(ckv_ref[...], gkv_ref[...]).astype(_MXU_DTYPE)
    kv = _mm(kvn, wkv_ref[...])
    k_rope = _rotate(kr_ref[...], cos_t, sin_t, half, MLA_ROPE).astype(k_ref.dtype)
    lane = lax.broadcasted_iota(jnp.int32, (q.shape[0], LANES), 1)
    H = MLA_HEADS
    for pair in range(H // 2):
        c0 = H * MLA_NOPE + pair * LANES
        r = _rotate(q[:, c0:c0 + LANES], cos_t, sin_t, half, MLA_ROPE)
        r_even = jnp.where(lane < MLA_ROPE, r, 0.0)
        r_odd = jnp.where(lane < MLA_ROPE, pltpu.roll(r, MLA_ROPE, 1), 0.0)
        for h, rr in ((2 * pair, r_even), (2 * pair + 1, r_odd)):
            q_ref[:, h * MLA_QK:h * MLA_QK + LANES] = q[:, h * MLA_NOPE:(h + 1) * MLA_NOPE].astype(q_ref.dtype)
            q_ref[:, h * MLA_QK + LANES:(h + 1) * MLA_QK] = rr.astype(q_ref.dtype)
    for h in range(H):
        k_ref[:, h * MLA_QK:h * MLA_QK + LANES] = kv[:, h * MLA_NOPE:(h + 1) * MLA_NOPE].astype(k_ref.dtype)
        k_ref[:, h * MLA_QK + LANES:(h + 1) * MLA_QK] = k_rope
    v_ref[...] = kv[:, H * MLA_NOPE:].astype(v_ref.dtype)


def _mla_up(p, pos, invf, sign, gq, gkv, wq, wkv, S):
    N = p.shape[0]
    tm = _pick(S, 512, 8)
    H = MLA_HEADS
    kern = functools.partial(_mla_up_kernel, scale=(MLA_NOPE + MLA_ROPE) ** -0.5)
    row = lambda i: (0, 0)
    return pl.pallas_call(
        kern,
        grid=(N // tm,),
        in_specs=[
            pl.BlockSpec((tm, MLA_Q_LORA), lambda i: (i, COL_CQ // MLA_Q_LORA)),
            pl.BlockSpec((tm, MLA_KV_LORA), lambda i: (i, COL_CKV // MLA_KV_LORA)),
            pl.BlockSpec((tm, LANES), lambda i: (i, COL_KR // LANES)),
            pl.BlockSpec((tm, 1), lambda i: (i, 0)),
            pl.BlockSpec((1, LANES), row),
            pl.BlockSpec((1, LANES), row),
            pl.BlockSpec((1, MLA_Q_LORA), row),
            pl.BlockSpec((1, MLA_KV_LORA), row),
            pl.BlockSpec(wq.shape, row),
            pl.BlockSpec(wkv.shape, row),
        ],
        out_specs=[
            pl.BlockSpec((tm, H * MLA_QK), lambda i: (i, 0)),
            pl.BlockSpec((tm, H * MLA_QK), lambda i: (i, 0)),
            pl.BlockSpec((tm, H * MLA_V), lambda i: (i, 0)),
        ],
        out_shape=[
            jax.ShapeDtypeStruct((N, H * MLA_QK), _MXU_DTYPE),
            jax.ShapeDtypeStruct((N, H * MLA_QK), _MXU_DTYPE),
            jax.ShapeDtypeStruct((N, H * MLA_V), _MXU_DTYPE),
        ],
        compiler_params=_cparams(("arbitrary",)),
        name="mla_up",
    )(p, p, p, pos, invf, sign, gq, gkv, wq, wkv)


def _softmax_init(m_ref, l_ref, acc_ref):
    m_ref[...] = jnp.full(m_ref.shape, NEG, f32)
    l_ref[...] = jnp.zeros(l_ref.shape, f32)
    acc_ref[...] = jnp.zeros(acc_ref.shape, f32)


def _softmax_step(load_s, mask_fn, v, m_ref, l_ref, a_ref, p_ref, acc_ref, rc):
    rows = m_ref.shape[0]
    for r0 in range(0, rows, rc):
        rs = slice(r0, r0 + rc)
        s = load_s(r0)
        msk = None if mask_fn is None else mask_fn(r0)
        if msk is not None:
            s = jnp.where(msk, s, NEG)
        nc = s.shape[1] // LANES
        cols = [s[:, c * LANES:(c + 1) * LANES] for c in range(nc)]
        m = m_ref[rs]
        m_new = jnp.maximum(m, jnp.max(functools.reduce(jnp.maximum, cols), axis=1, keepdims=True))
        ps = [jnp.exp(cc - m_new) for cc in cols]
        alpha = jnp.exp(m - m_new)
        l_ref[rs] = alpha * l_ref[rs] + jnp.sum(functools.reduce(jnp.add, ps), axis=1, keepdims=True)
        m_ref[rs] = m_new
        a_ref[rs] = alpha
        for c in range(nc):
            p_ref[rs, c * LANES:(c + 1) * LANES] = ps[c].astype(p_ref.dtype)
    acc_ref[...] = a_ref[...] * acc_ref[...] + _mm(p_ref[...], v)


def _flash_kernel(*refs, tq, tk, rc, scale, decay):
    if decay:
        q_ref, k_ref, v_ref, cq_ref, ck_ref, o_ref, s_ref, p_ref, m_ref, l_ref, a_ref, acc_ref = refs
    else:
        q_ref, k_ref, v_ref, o_ref, s_ref, p_ref, m_ref, l_ref, a_ref, acc_ref = refs
    i = pl.program_id(2)
    q = q_ref[...]
    if scale != 1.0:
        q = q.astype(f32) * scale
    q = q.astype(_MXU_DTYPE)
    if decay:
        cq = cq_ref[...]

    def produce(kt, slot):
        k0 = pl.multiple_of(kt * tk, tk)
        s = _mm_nt(q, k_ref[pl.ds(k0, tk), :].astype(_MXU_DTYPE))
        if decay:
            ck = ck_ref[0, 0, :, pl.ds(k0, tk)]
            for c in range(tk // LANES):
                cs = slice(c * LANES, (c + 1) * LANES)
                s_ref[slot, :, cs] = (s[:, cs] - ck[:, cs]) + cq
        else:
            s_ref[slot] = s

    def consume(kt, slot, masked):
        k0 = pl.multiple_of(kt * tk, tk)
        mask_fn = None
        if masked:
            col = k0 + lax.broadcasted_iota(jnp.int32, (1, tk), 1)
            mask_fn = lambda r0: col <= (i * tq + r0 + lax.broadcasted_iota(jnp.int32, (rc, 1), 0))
        _softmax_step(lambda r0: s_ref[slot, r0:r0 + rc, :], mask_fn, v_ref[pl.ds(k0, tk), :].astype(_MXU_DTYPE),
                      m_ref, l_ref, a_ref, p_ref, acc_ref, rc)

    _softmax_init(m_ref, l_ref, acc_ref)
    nd = tq // tk
    produce(0, 0)

    def pair(j, carry):
        kt = 2 * j
        produce(kt + 1, 1)
        consume(kt, 0, False)
        produce(kt + 2, 0)
        consume(kt + 1, 1, False)
        return carry

    lax.fori_loop(0, i * (nd // 2), pair, 0)
    for d in range(nd):
        kt = i * nd + d
        if d + 1 < nd:
            produce(kt + 1, (d + 1) % 2)
        consume(kt, d % 2, True)
    o_ref[...] = (acc_ref[...] / l_ref[...]).astype(o_ref.dtype)


def _flash(q, k, v, *, B, S, H, dk, dv, qcol, kcol, vcol, scale, cum_col=None, cum_row=None):
    N = B * S
    tq = _pick(S, 512, 8)
    tk = tq // 2
    rc = _pick(tq, 128, 8)
    nq = S // tq
    decay = cum_col is not None
    in_specs = [
        pl.BlockSpec((tq, dk), lambda b, h, i: (b * nq + i, qcol + h)),
        pl.BlockSpec((S, dk), lambda b, h, i: (b, kcol + h)),
        pl.BlockSpec((S, dv), lambda b, h, i: (b, vcol + h)),
    ]
    args = [q, k, v]
    if decay:
        in_specs += [
            pl.BlockSpec((tq, LANES), lambda b, h, i: (b * nq + i, h)),
            pl.BlockSpec((1, 1, 1, S), lambda b, h, i: (b, h, 0, 0)),
        ]
        args += [cum_col, cum_row]
    kern = functools.partial(_flash_kernel, tq=tq, tk=tk, rc=rc, scale=scale, decay=decay)
    return pl.pallas_call(
        kern,
        grid=(B, H, nq),
        in_specs=in_specs,
        out_specs=pl.BlockSpec((tq, dv), lambda b, h, i: (b * nq + i, h)),
        out_shape=jax.ShapeDtypeStruct((N, H * dv), f32),
        scratch_shapes=[
            pltpu.VMEM((2, tq, tk), f32),
            pltpu.VMEM((tq, tk), _MXU_DTYPE),
            pltpu.VMEM((tq, LANES), f32),
            pltpu.VMEM((tq, LANES), f32),
            pltpu.VMEM((tq, LANES), f32),
            pltpu.VMEM((tq, dv), f32),
        ],
        compiler_params=_cparams(("arbitrary", "arbitrary", "arbitrary")),
        name="flash_fox" if decay else "flash_mla",
    )(*args)


def _fox_prep_kernel(ff_ref, b_ref, o_ref, rep_ref, *, chunk):
    S = ff_ref.shape[0]
    r = lax.broadcasted_iota(jnp.int32, (chunk, chunk), 0)
    c = lax.broadcasted_iota(jnp.int32, (chunk, chunk), 1)
    tri = (c <= r).astype(_MXU_DTYPE)

    def body(j, carry):
        r0 = pl.multiple_of(j * chunk, chunk)
        x = ff_ref[pl.ds(r0, chunk), :] + b_ref[...]
        lf = jnp.minimum(x, 0.0) - jnp.log(1.0 + jnp.exp(-jnp.abs(x)))
        hi, mid, lo = _split3(lf)
        cum = (_mm(tri, hi) + _mm(tri, mid)) + _mm(tri, lo) + carry
        o_ref[pl.ds(r0, chunk), :] = cum
        for h in range(FOX_HEADS):
            rep_ref[pl.ds(r0, chunk), h * LANES:(h + 1) * LANES] = jnp.broadcast_to(cum[:, h:h + 1], (chunk, LANES))
        return cum[chunk - 1:chunk, :]

    lax.fori_loop(0, S // chunk, body, jnp.zeros((1, LANES), f32))


def _fox_prep(p, b_pad, B, S):
    N = B * S
    kern = functools.partial(_fox_prep_kernel, chunk=_pick(S, 128, 8))
    return pl.pallas_call(
        kern,
        grid=(B,),
        in_specs=[
            pl.BlockSpec((S, LANES), lambda b: (b, COL_FF // LANES)),
            pl.BlockSpec((1, LANES), lambda b: (0, 0)),
        ],
        out_specs=[pl.BlockSpec((S, LANES), lambda b: (b, 0)),
                   pl.BlockSpec((S, FOX_HEADS * LANES), lambda b: (b, 0))],
        out_shape=[jax.ShapeDtypeStruct((N, LANES), f32),
                   jax.ShapeDtypeStruct((N, FOX_HEADS * LANES), f32)],
        compiler_params=_cparams(("arbitrary",)),
        name="fox_prep",
    )(p, b_pad)


def _nsa_prep_kernel(q_ref, kc_ref, vc_ref, ks_ref, vs_ref, kw_ref, vw_ref, pos_ref, invf_ref, sign_ref,
                     qo_ref, cmp_ref, kso_ref, vso_ref, kwo_ref, vwo_ref, kc_scr, *, scale):
    cos_t, sin_t = _rot_tables(pos_ref, invf_ref, sign_ref)
    half = PARTIAL_ROT // 2
    rot = lambda x: _rotate(x, cos_t, sin_t, half, LANES)
    for h in range(NSA_HEADS):
        sl = slice(h * HEAD_DIM, (h + 1) * HEAD_DIM)
        qo_ref[:, sl] = (rot(q_ref[:, sl]) * scale).astype(qo_ref.dtype)
    nrow = cmp_ref.shape[2]
    for g in range(NSA_KV_GROUPS):
        sl = slice(g * HEAD_DIM, (g + 1) * HEAD_DIM)
        kso_ref[:, sl] = rot(ks_ref[:, sl]).astype(kso_ref.dtype)
        kwo_ref[:, sl] = rot(kw_ref[:, sl]).astype(kwo_ref.dtype)
        kc_scr[0] = rot(kc_ref[:, sl])
        kc_scr[1] = vc_ref[:, sl]
        for t in range(NSA_CMP_STRIDE):
            cs = slice(t * HEAD_DIM, (t + 1) * HEAD_DIM)
            for kv in range(2):
                cmp_ref[kv, g, :, cs] = kc_scr[kv, pl.ds(t, nrow, stride=NSA_CMP_STRIDE), :].astype(cmp_ref.dtype)
    vso_ref[...] = vs_ref[...].astype(vso_ref.dtype)
    vwo_ref[...] = vw_ref[...].astype(vwo_ref.dtype)


def _nsa_prep(p, pos, invf, sign, S):
    N = p.shape[0]
    tm = _pick(S, 512, 8)
    G = NSA_KV_GROUPS
    cw = NSA_CMP_STRIDE * HEAD_DIM
    kv = lambda col: pl.BlockSpec((tm, NSA_KV), lambda i: (i, col // NSA_KV))
    row = lambda i: (0, 0)
    kern = functools.partial(_nsa_prep_kernel, scale=HEAD_DIM ** -0.5)
    return pl.pallas_call(
        kern,
        grid=(N // tm,),
        in_specs=[
            pl.BlockSpec((tm, W_NSA), lambda i: (i, COL_NQ // W_NSA)),
            kv(COL_NKC), kv(COL_NVC), kv(COL_NKS), kv(COL_NVS), kv(COL_NKW), kv(COL_NVW),
            pl.BlockSpec((tm, 1), lambda i: (i, 0)),
            pl.BlockSpec((1, LANES), row),
            pl.BlockSpec((1, LANES), row),
        ],
        out_specs=[
            pl.BlockSpec((tm, W_NSA), lambda i: (i, 0)),
            pl.BlockSpec((2, G, tm // NSA_CMP_STRIDE, cw), lambda i: (0, 0, i, 0)),
            pl.BlockSpec((tm, NSA_KV), lambda i: (i, 0)),
            pl.BlockSpec((tm, NSA_KV), lambda i: (i, 0)),
            pl.BlockSpec((tm, NSA_KV), lambda i: (i, 0)),
            pl.BlockSpec((tm, NSA_KV), lambda i: (i, 0)),
        ],
        out_shape=[
            jax.ShapeDtypeStruct((N, W_NSA), _MXU_DTYPE),
            jax.ShapeDtypeStruct((2, G, N // NSA_CMP_STRIDE, cw), _MXU_DTYPE),
            jax.ShapeDtypeStruct((N, NSA_KV), _MXU_DTYPE),
            jax.ShapeDtypeStruct((N, NSA_KV), _MXU_DTYPE),
            jax.ShapeDtypeStruct((N, NSA_KV), _MXU_DTYPE),
            jax.ShapeDtypeStruct((N, NSA_KV), _MXU_DTYPE),
        ],
        scratch_shapes=[pltpu.VMEM((2, tm, HEAD_DIM), f32)],
        compiler_params=_cparams(("arbitrary",)),
        name="nsa_prep",
    )(p, p, p, p, p, p, p, pos, invf, sign)


def _compress_kernel(c_ref, pos_ref, w1_ref, w2_ref, o_ref):
    x = c_ref[0, 0, 0]
    half = x.shape[1]
    nrow = x.shape[0]
    a = _mm(x, w1_ref[0, :half, :])
    b = _mm(x, w1_ref[0, half:, :])
    posb = _mm(pos_ref[0].astype(_MXU_DTYPE), w1_ref[0])[0:1, :]
    hid = a + pltpu.roll(b, nrow - 1, 0) + posb
    c0 = math.sqrt(2.0 / math.pi)
    act = 0.5 * hid * (1.0 + jnp.tanh(c0 * (hid + 0.044715 * (hid * hid * hid))))
    o_ref[0, 0, 0] = _mm(act.astype(_MXU_DTYPE), w2_ref[0])


def _compress(cmp_in, pos_flat, w1, w2, B, S):
    G = NSA_KV_GROUPS
    nch = S // NSA_CMP_STRIDE
    width = NSA_CMP_STRIDE * HEAD_DIM
    x = cmp_in.reshape(2, G, B, nch, width)
    return pl.pallas_call(
        _compress_kernel,
        grid=(2, G, B),
        in_specs=[
            pl.BlockSpec((1, 1, 1, nch, width), lambda t, g, b: (t, g, b, 0, 0)),
            pl.BlockSpec((1, 8, 2 * width), lambda t, g, b: (t, 0, 0)),
            pl.BlockSpec((1, 2 * width, NSA_CMP_HIDDEN), lambda t, g, b: (t, 0, 0)),
            pl.BlockSpec((1, NSA_CMP_HIDDEN, HEAD_DIM), lambda t, g, b: (t, 0, 0)),
        ],
        out_specs=pl.BlockSpec((1, 1, 1, nch, HEAD_DIM), lambda t, g, b: (t, g, b, 0, 0)),
        out_shape=jax.ShapeDtypeStruct((2, G, B, nch, HEAD_DIM), f32),
        compiler_params=_cparams(("arbitrary", "arbitrary", "arbitrary")),
        name="nsa_compress",
    )(x, pos_flat, w1, w2)


def _nsa_attn_kernel(q_ref, kc_ref, vc_ref, ks_ref, vs_ref, kw_ref, vw_ref, g_ref, ovl_ref, exp_ref, o_ref,
                     s_ref, p_ref, m_ref, l_ref, a_ref, acc_ref, out_scr, *, tq, tk, n_slc, top_k, nw_max):
    i = pl.program_id(2)
    Hg = NSA_GQA
    D = HEAD_DIM
    q4 = q_ref[...]
    qs = jnp.concatenate([q4[:, h * D:(h + 1) * D] for h in range(Hg)], axis=0)
    t = i * tq + lax.broadcasted_iota(jnp.int32, (tq, 1), 0)
    gate = jax.nn.sigmoid(g_ref[...])
    head_rows = lambda h: slice(h * tq, (h + 1) * tq)

    kc = kc_ref[0, 0, 0].astype(_MXU_DTYPE)
    vc = vc_ref[0, 0, 0].astype(_MXU_DTYPE)
    ncmp = kc.shape[0]
    cend = lax.broadcasted_iota(jnp.int32, (1, ncmp), 1) * NSA_CMP_STRIDE + (NSA_CMP_LEN - 1)
    valid = cend <= t
    psum = None
    for h in range(Hg):
        s = jnp.where(valid, _mm_nt(q4[:, h * D:(h + 1) * D], kc), NEG)
        e = jnp.where(valid, jnp.exp(s - jnp.max(s, axis=1, keepdims=True)), 0.0)
        l = jnp.sum(e, axis=1, keepdims=True)
        p = e / jnp.where(l > 0.0, l, 1.0)
        out_scr[head_rows(h)] = gate[:, 3 * h:3 * h + 1] * _mm(p.astype(_MXU_DTYPE), vc)
        psum = p if psum is None else psum + p

    hi, mid, lo = _split3(psum)
    ovl = ovl_ref[...]
    imp = (_mm(hi, ovl) + _mm(mid, ovl)) + _mm(lo, ovl)
    nb = -(-n_slc // 8) * 8
    x = imp.T[:nb]
    jr = lax.broadcasted_iota(jnp.int32, (nb, 1), 0)
    cur_l = (i * tq + lax.broadcasted_iota(jnp.int32, (1, tq), 1)) // NSA_SLC_BLOCK
    forced = (jr == 0) | (jr == cur_l) | (jr == cur_l - 1)
    x = jnp.where(forced, BIG, jnp.where(jr > cur_l, NEG, x))
    rank = jnp.zeros((nb, tq), f32)
    for kk in range(n_slc):
        ck = x[kk:kk + 1, :]
        before = (ck > x) | ((ck == x) & (jr > kk))
        rank = rank + before.astype(f32)
    sel_t = ((rank < float(top_k)) & (jr <= cur_l)).astype(f32)
    if nb < LANES:
        sel_t = jnp.concatenate([sel_t, jnp.zeros((LANES - nb, tq), f32)], axis=0)
    sel = sel_t.T.astype(_MXU_DTYPE)

    def run_step(slot, k0, mask, v_ref):
        _softmax_step(lambda r0: s_ref[slot, r0:r0 + tq, :], lambda r0: mask, v_ref[pl.ds(k0, tk), :],
                      m_ref, l_ref, a_ref, p_ref, acc_ref, tq)

    def add_gated(c):
        o = acc_ref[...] / l_ref[...]
        for h in range(Hg):
            out_scr[head_rows(h)] += gate[:, 3 * h + c:3 * h + c + 1] * o[head_rows(h)]

    def slc_produce(kt, slot):
        k0 = pl.multiple_of(kt * tk, tk)
        s_ref[slot] = _mm_nt(qs, ks_ref[pl.ds(k0, tk), :])

    def slc_consume(kt, slot):
        k0 = pl.multiple_of(kt * tk, tk)
        chosen = _mm(sel, exp_ref[:, pl.ds(k0, tk)])
        col = k0 + lax.broadcasted_iota(jnp.int32, (1, tk), 1)
        run_step(slot, k0, (chosen > 0.5) & (col <= t), vs_ref)

    _softmax_init(m_ref, l_ref, acc_ref)
    n_t = ((i + 1) * tq + tk - 1) // tk
    odd = n_t % 2

    @pl.when(odd == 1)
    def _():
        slc_produce(0, 0)
        slc_consume(0, 0)

    slc_produce(jnp.minimum(odd, n_t - 1), 0)

    def slc_pair(j, carry):
        kt = odd + 2 * j
        slc_produce(jnp.minimum(kt + 1, n_t - 1), 1)
        slc_consume(kt, 0)
        slc_produce(jnp.minimum(kt + 2, n_t - 1), 0)
        slc_consume(kt + 1, 1)
        return carry

    lax.fori_loop(0, n_t // 2, slc_pair, 0)
    add_gated(1)

    def win_produce(kt, slot):
        k0 = pl.multiple_of(kt * tk, tk)
        s_ref[slot] = _mm_nt(qs, kw_ref[pl.ds(k0, tk), :])

    def win_consume(kt, slot):
        k0 = pl.multiple_of(kt * tk, tk)
        col = k0 + lax.broadcasted_iota(jnp.int32, (1, tk), 1)
        run_step(slot, k0, (col <= t) & (t - col < NSA_WINDOW), vw_ref)

    _softmax_init(m_ref, l_ref, acc_ref)
    hi_t = (i * tq + tq - 1) // tk
    lo_t = jnp.maximum(i * tq - (NSA_WINDOW - 1), 0) // tk
    win_produce(hi_t, 0)

    def win_tail(d):
        if d >= nw_max:
            return

        @pl.when(hi_t - d >= lo_t)
        def _():
            if d + 1 < nw_max:
                win_produce(jnp.maximum(hi_t - d - 1, 0), (d + 1) % 2)
            win_consume(hi_t - d, d % 2)
            win_tail(d + 1)

    win_tail(0)
    add_gated(2)
    for h in range(Hg):
        o_ref[:, h * D:(h + 1) * D] = out_scr[head_rows(h)].astype(o_ref.dtype)


def _nsa_attn(qn, cmp_kv, ks, vs, kw, vw, p, ovl, expand, B, S):
    N = B * S
    G = NSA_KV_GROUPS
    D = HEAD_DIM
    tq = LANES
    tk = _pick(S, 256, 8)
    nq = S // tq
    nch = cmp_kv.shape[3]
    n_slc = S // NSA_SLC_BLOCK
    assert n_slc <= LANES
    nw_max = max((i * tq + tq - 1) // tk - max(i * tq - (NSA_WINDOW - 1), 0) // tk + 1 for i in range(nq))
    kern = functools.partial(_nsa_attn_kernel, tq=tq, tk=tk, n_slc=n_slc, top_k=min(NSA_TOPK, n_slc),
                             nw_max=nw_max)
    R = NSA_GQA * tq
    seq = lambda: pl.BlockSpec((S, D), lambda b, g, i: (b, g))
    return pl.pallas_call(
        kern,
        grid=(B, G, nq),
        in_specs=[
            pl.BlockSpec((tq, NSA_GQA * D), lambda b, g, i: (b * nq + i, g)),
            pl.BlockSpec((1, 1, 1, nch, D), lambda b, g, i: (0, g, b, 0, 0)),
            pl.BlockSpec((1, 1, 1, nch, D), lambda b, g, i: (1, g, b, 0, 0)),
            seq(), seq(), seq(), seq(),
            pl.BlockSpec((tq, LANES), lambda b, g, i: (b * nq + i, COL_NG // LANES + g)),
            pl.BlockSpec(ovl.shape, lambda b, g, i: (0, 0)),
            pl.BlockSpec(expand.shape, lambda b, g, i: (0, 0)),
        ],
        out_specs=pl.BlockSpec((tq, NSA_GQA * D), lambda b, g, i: (b * nq + i, g)),
        out_shape=jax.ShapeDtypeStruct((N, W_NSA), f32),
        scratch_shapes=[
            pltpu.VMEM((2, R, tk), f32),
            pltpu.VMEM((R, tk), _MXU_DTYPE),
            pltpu.VMEM((R, LANES), f32),
            pltpu.VMEM((R, LANES), f32),
            pltpu.VMEM((R, LANES), f32),
            pltpu.VMEM((R, D), f32),
            pltpu.VMEM((R, D), f32),
        ],
        compiler_params=_cparams(("arbitrary", "arbitrary", "arbitrary")),
        name="nsa_attn",
    )(qn, cmp_kv, cmp_kv, ks, vs, kw, vw, p, ovl, expand)


def _nsa_consts(S):
    nch = S // NSA_CMP_STRIDE
    n_slc = S // NSA_SLC_BLOCK
    c_start = np.arange(nch) * NSA_CMP_STRIDE
    s_start = np.arange(n_slc) * NSA_SLC_BLOCK
    ov = (np.minimum(c_start[:, None] + NSA_CMP_LEN, s_start[None, :] + NSA_SLC_BLOCK)
          - np.maximum(c_start[:, None], s_start[None, :]))
    ovl = np.zeros((nch, LANES), np.float32)
    ovl[:, :n_slc] = np.clip(ov, 0, None) / NSA_CMP_LEN
    expand = np.zeros((LANES, S), np.float32)
    expand[np.arange(S) // NSA_SLC_BLOCK, np.arange(S)] = 1.0
    return jnp.asarray(ovl, _MXU_DTYPE), jnp.asarray(expand, _MXU_DTYPE)


def _out_proj_kernel(om_ref, of_ref, on_ref, gm_ref, gf_ref, gn_ref, w_ref, x_ref, gt_ref, o_ref, h_ref):
    @pl.when(pl.program_id(1) == 0)
    def _():
        h_ref[:, :W_MLA] = _rms(om_ref[...], gm_ref[...]).astype(h_ref.dtype)
        h_ref[:, W_MLA:W_MLA + W_FOX] = _rms(of_ref[...], gf_ref[...]).astype(h_ref.dtype)
        h_ref[:, W_MLA + W_FOX:] = _rms(on_ref[...], gn_ref[...]).astype(h_ref.dtype)

    o_ref[...] = x_ref[...] + gt_ref[0] * _mm(h_ref[...], w_ref[...])


def _out_proj(o_mla, o_fox, o_nsa, gm, gf, gn, w, x2, mod3, S):
    N, D = x2.shape
    DM = w.shape[0]
    tm = _pick(S, 512, 8)
    tn = _pick(D, 1024)
    nj = D // tn
    row = lambda i, j: (0, 0)
    return pl.pallas_call(
        _out_proj_kernel,
        grid=(N // tm, nj),
        in_specs=[
            pl.BlockSpec((tm, W_MLA), lambda i, j: (i, 0)),
            pl.BlockSpec((tm, W_FOX), lambda i, j: (i, 0)),
            pl.BlockSpec((tm, W_NSA), lambda i, j: (i, 0)),
            pl.BlockSpec((1, W_MLA), row),
            pl.BlockSpec((1, W_FOX), row),
            pl.BlockSpec((1, W_NSA), row),
            pl.BlockSpec((DM, tn), lambda i, j: (0, j)),
            pl.BlockSpec((tm, tn), lambda i, j: (i, j)),
            pl.BlockSpec((1, 1, tn), lambda i, j: (i * tm // S, 0, 2 * nj + j)),
        ],
        out_specs=pl.BlockSpec((tm, tn), lambda i, j: (i, j)),
        out_shape=jax.ShapeDtypeStruct((N, D), f32),
        scratch_shapes=[pltpu.VMEM((tm, DM), _MXU_DTYPE)],
        compiler_params=_cparams(("arbitrary", "arbitrary")),
        name="out_proj",
    )(o_mla, o_fox, o_nsa, gm, gf, gn, w, x2, mod3)


def _mlp_kernel(x_ref, sh_ref, sc_ref, gt_ref, g_ref, w1_ref, w2_ref, fg_ref, o_ref, h_ref, acc_ref, *, final):
    f = pl.program_id(1)

    @pl.when(f == 0)
    def _():
        h = _rms(x_ref[...], g_ref[...]) * (1.0 + sc_ref[0]) + sh_ref[0]
        h_ref[...] = h.astype(h_ref.dtype)
        acc_ref[...] = jnp.zeros_like(acc_ref)

    u = jnp.maximum(_mm(h_ref[...], w1_ref[...]), 0.0)
    acc_ref[...] += _mm((u * u).astype(_MXU_DTYPE), w2_ref[...])

    @pl.when(f == pl.num_programs(1) - 1)
    def _():
        y = x_ref[...] + gt_ref[0] * acc_ref[...]
        if final:
            y = _rms(y, fg_ref[...])
        o_ref[...] = y


def _mlp(x2, mod3, gain, w1, w2, final_gain, S, final):
    N, D = x2.shape
    F = w1.shape[1]
    tm = _pick(S, 512, 8)
    tf = _pick(F, 512)
    row = lambda i, f: (0, 0)
    mod_blk = lambda c: pl.BlockSpec((1, 1, D), lambda i, f: (i * tm // S, 0, c))
    return pl.pallas_call(
        functools.partial(_mlp_kernel, final=final),
        grid=(N // tm, F // tf),
        in_specs=[
            pl.BlockSpec((tm, D), lambda i, f: (i, 0)),
            mod_blk(3), mod_blk(4), mod_blk(5),
            pl.BlockSpec((1, D), row),
            pl.BlockSpec((D, tf), lambda i, f: (0, f)),
            pl.BlockSpec((tf, D), lambda i, f: (f, 0)),
            pl.BlockSpec((1, D), row),
        ],
        out_specs=pl.BlockSpec((tm, D), lambda i, f: (i, 0)),
        out_shape=jax.ShapeDtypeStruct((N, D), f32),
        scratch_shapes=[pltpu.VMEM((tm, D), _MXU_DTYPE), pltpu.VMEM((tm, D), f32)],
        compiler_params=_cparams(("arbitrary", "arbitrary")),
        name="mlp",
    )(x2, mod3, mod3, mod3, gain, w1, w2, final_gain)


def _w_in_segments():
    widths = (MLA_Q_LORA, MLA_KV_LORA, MLA_ROPE, W_FOX, W_FOX, W_FOX, FOX_HEADS,
              W_NSA, NSA_KV, NSA_KV, NSA_KV, NSA_KV, NSA_KV, NSA_KV, 3 * NSA_HEADS)
    src = np.cumsum((0,) + widths)
    (cq, ckv, kr, fq, fk, fv, ff, nq, nkc, nvc, nks, nvs, nkw, nvw, ng) = [int(s) for s in src[:-1]]
    gw = 3 * NSA_GQA
    return [(COL_NQ, nq, W_NSA), (COL_CQ, cq, MLA_Q_LORA), (COL_FQ, fq, W_FOX), (COL_FK, fk, W_FOX),
            (COL_FV, fv, W_FOX), (COL_CKV, ckv, MLA_KV_LORA), (COL_NKC, nkc, NSA_KV), (COL_NVC, nvc, NSA_KV),
            (COL_NKS, nks, NSA_KV), (COL_NVS, nvs, NSA_KV), (COL_NKW, nkw, NSA_KV), (COL_NVW, nvw, NSA_KV),
            (COL_KR, kr, MLA_ROPE), (COL_FF, ff, FOX_HEADS), (COL_NG, ng, gw), (COL_NG + LANES, ng + gw, gw)]


def _w_in_regroup_kernel(w_ref, o_ref):
    o_ref[...] = jnp.zeros(o_ref.shape, o_ref.dtype)
    for dst, src, width in _w_in_segments():
        o_ref[:, dst:dst + width] = w_ref[0, :, src:src + width].astype(o_ref.dtype)


def _prep_w_in(w_in, l):
    _, D, C = w_in.shape
    tr = _pick(D, 256, 8)
    return pl.pallas_call(
        _w_in_regroup_kernel,
        grid=(D // tr,),
        in_specs=[pl.BlockSpec((1, tr, C), lambda i: (l, i, 0))],
        out_specs=pl.BlockSpec((tr, N_COLS), lambda i: (i, 0)),
        out_shape=jax.ShapeDtypeStruct((D, N_COLS), _MXU_DTYPE),
        compiler_params=_cparams(("arbitrary",)),
        name="w_in_regroup",
    )(w_in)


def _prep_mla_w(w_uq, w_ukv):
    H = MLA_HEADS
    qd = MLA_NOPE + MLA_ROPE
    q_nope = np.concatenate([np.arange(h * qd, h * qd + MLA_NOPE) for h in range(H)])
    q_rope = np.concatenate([np.arange(h * qd + MLA_NOPE, (h + 1) * qd) for h in range(H)])
    kd = MLA_NOPE + MLA_V
    k_nope = np.concatenate([np.arange(h * kd, h * kd + MLA_NOPE) for h in range(H)])
    v_cols = np.concatenate([np.arange(h * kd + MLA_NOPE, (h + 1) * kd) for h in range(H)])
    wq = jnp.take(w_uq, jnp.asarray(np.concatenate([q_nope, q_rope])), axis=1).astype(_MXU_DTYPE)
    wkv = jnp.take(w_ukv, jnp.asarray(np.concatenate([k_nope, v_cols])), axis=1).astype(_MXU_DTYPE)
    return wq, wkv


def _pad_lanes(v):
    return jnp.zeros((1, LANES), f32).at[0, :v.shape[0]].set(v.astype(f32))


def kernel(x, c, positions, ada_w, ada_b, norm_mix, norm_mlp, w_in, fox_b_f, mla_q_norm, mla_w_uq, mla_kv_norm,
           mla_w_ukv, nsa_pos_k, nsa_pos_v, nsa_cmp_k_w1, nsa_cmp_k_w2, nsa_cmp_v_w1, nsa_cmp_v_w2, out_norm,
           w_out, mlp_w1, mlp_w2, final_norm):
    B, S, D = x.shape
    L = ada_w.shape[0]
    N = B * S
    assert S % LANES == 0 and D % LANES == 0 and B <= 8

    x2 = x.reshape(N, D)
    pos = positions.reshape(N, 1).astype(jnp.int32)
    c_pad = jnp.zeros((8, D), f32).at[:B].set(c)
    mod = _ada_mod(c_pad, ada_w, ada_b)

    invf_mla, sign_mla = _rope_consts(MLA_ROPE // 2, MLA_ROPE, LANES)
    invf_nsa, sign_nsa = _rope_consts(PARTIAL_ROT // 2, LANES, LANES)
    ovl, expand = _nsa_consts(S)
    row = lambda v: v.reshape(1, -1)

    for l in range(L):
        mod3 = mod[l, :B].reshape(B, 1, 6 * D)
        p = _proj_in(x2, mod3, row(norm_mix[l]), _prep_w_in(w_in, l), S)

        wq, wkv = _prep_mla_w(mla_w_uq[l], mla_w_ukv[l])
        q_m, k_m, v_m = _mla_up(p, pos, invf_mla, sign_mla, row(mla_q_norm[l]), row(mla_kv_norm[l]), wq, wkv, S)
        o_mla = _flash(q_m, k_m, v_m, B=B, S=S, H=MLA_HEADS, dk=MLA_QK, dv=MLA_V, qcol=0, kcol=0, vcol=0,
                       scale=1.0)

        cum, cum_rep = _fox_prep(p, _pad_lanes(fox_b_f[l]), B, S)
        cum_row = cum[:, :FOX_HEADS].reshape(B, S, FOX_HEADS).transpose(0, 2, 1)[:, :, None, :]
        o_fox = _flash(p, p, p, B=B, S=S, H=FOX_HEADS, dk=HEAD_DIM, dv=HEAD_DIM,
                       qcol=COL_FQ // HEAD_DIM, kcol=COL_FK // HEAD_DIM, vcol=COL_FV // HEAD_DIM,
                       scale=HEAD_DIM ** -0.5, cum_col=cum_rep, cum_row=cum_row)

        qn, cmp_in, ks, vs, kw, vw = _nsa_prep(p, pos, invf_nsa, sign_nsa, S)
        flat = lambda a: jnp.broadcast_to(a.reshape(1, -1), (8, a.size))
        cmp_kv = _compress(
            cmp_in,
            jnp.stack([flat(nsa_pos_k[l]), flat(nsa_pos_v[l])]).astype(f32),
            jnp.stack([nsa_cmp_k_w1[l], nsa_cmp_v_w1[l]]).astype(_MXU_DTYPE),
            jnp.stack([nsa_cmp_k_w2[l], nsa_cmp_v_w2[l]]).astype(_MXU_DTYPE), B, S)
        o_nsa = _nsa_attn(qn, cmp_kv, ks, vs, kw, vw, p, ovl, expand, B, S)

        gn = out_norm[l]
        x2 = _out_proj(o_mla, o_fox, o_nsa, row(gn[:W_MLA]), row(gn[W_MLA:W_MLA + W_FOX]),
                       row(gn[W_MLA + W_FOX:]), w_out[l].astype(_MXU_DTYPE), x2, mod3, S)
        x2 = _mlp(x2, mod3, row(norm_mlp[l]), mlp_w1[l].astype(_MXU_DTYPE), mlp_w2[l].astype(_MXU_DTYPE),
                  row(final_norm), S, final=(l == L - 1))
    return x2.reshape(B, S, D)
```

```python
import functools
import math

import numpy as np
import jax
import jax.numpy as jnp
from jax import lax
from jax.experimental import pallas as pl
from jax.experimental.pallas import tpu as pltpu

f32 = jnp.float32
_MXU_DTYPE = jnp.bfloat16

HEAD_DIM = 128
MLA_HEADS = 4
MLA_Q_LORA = 512
MLA_KV_LORA = 256
MLA_NOPE = 128
MLA_ROPE = 64
MLA_V = 128
FOX_HEADS = 4
NSA_HEADS = 8
NSA_KV_GROUPS = 2
NSA_GQA = NSA_HEADS // NSA_KV_GROUPS
NSA_CMP_LEN = 32
NSA_CMP_STRIDE = 16
NSA_CMP_HIDDEN = 512
NSA_SLC_BLOCK = 64
NSA_TOPK = 16
NSA_WINDOW = 512
W_MLA = MLA_HEADS * MLA_V
W_FOX = FOX_HEADS * HEAD_DIM
W_NSA = NSA_HEADS * HEAD_DIM
NSA_KV = NSA_KV_GROUPS * HEAD_DIM
ROPE_THETA = 500000.0
PARTIAL_ROT = HEAD_DIM // 4
EPS = 1e-6
NEG = -1e30
BIG = 1e30

LANES = 128
VMEM_LIMIT = 56 * 1024 * 1024

MLA_QK = 2 * LANES
COL_NQ = 0
COL_CQ = 1024
COL_FQ = 1536
COL_FK = 2048
COL_FV = 2560
COL_CKV = 3072
COL_NKC = 3328
COL_NVC = 3584
COL_NKS = 3840
COL_NVS = 4096
COL_NKW = 4352
COL_NVW = 4608
COL_KR = 4864
COL_FF = 4992
COL_NG = 5120
N_COLS = 5376


def _pick(n, cap, mult=LANES):
    if n <= cap:
        return n
    t = (cap // mult) * mult
    while t >= mult:
        if n % t == 0:
            return t
        t -= mult
    raise ValueError(f"no tile for {n} under {cap}")


def _mm(a, b):
    return jnp.dot(a, b, preferred_element_type=f32)


def _mm_nt(a, b):
    return lax.dot_general(a, b, (((1,), (1,)), ((), ())), preferred_element_type=f32)


def _split3(a):
    hi = a.astype(_MXU_DTYPE)
    r1 = a - hi.astype(f32)
    mid = r1.astype(_MXU_DTYPE)
    lo = (r1 - mid.astype(f32)).astype(_MXU_DTYPE)
    return hi, mid, lo


def _rms(x, gain):
    return x * lax.rsqrt(jnp.mean(x * x, axis=-1, keepdims=True) + EPS) * gain


def _cparams(sem):
    return pltpu.CompilerParams(dimension_semantics=sem, vmem_limit_bytes=VMEM_LIMIT)


def _ada_kernel(c_ref, w_ref, b_ref, o_ref):
    c = c_ref[...]
    ca = c * jax.nn.sigmoid(c)
    w = w_ref[0]
    c_hi, c_mid, _ = _split3(ca)
    w_hi, w_mid, _ = _split3(w)
    acc = _mm(c_hi, w_hi) + (_mm(c_mid, w_hi) + _mm(c_hi, w_mid))
    o_ref[0] = acc + b_ref[0]


def _ada_mod(c_pad, ada_w, ada_b):
    L, D, D6 = ada_w.shape
    tn = _pick(D6, 1024)
    return pl.pallas_call(
        _ada_kernel,
        grid=(L, D6 // tn),
        in_specs=[
            pl.BlockSpec((8, D), lambda l, j: (0, 0)),
            pl.BlockSpec((1, D, tn), lambda l, j: (l, 0, j)),
            pl.BlockSpec((1, 1, tn), lambda l, j: (l, 0, j)),
        ],
        out_specs=pl.BlockSpec((1, 8, tn), lambda l, j: (l, 0, j)),
        out_shape=jax.ShapeDtypeStruct((L, 8, D6), f32),
        compiler_params=_cparams(("arbitrary", "arbitrary")),
        name="ada_mod",
    )(c_pad, ada_w, ada_b.reshape(L, 1, D6))


def _proj_in_kernel(x_ref, sh_ref, sc_ref, g_ref, w_ref, o_ref, h_ref):
    @pl.when(pl.program_id(1) == 0)
    def _():
        h = _rms(x_ref[...], g_ref[...]) * (1.0 + sc_ref[0]) + sh_ref[0]
        h_ref[...] = h.astype(h_ref.dtype)

    o_ref[...] = _mm(h_ref[...], w_ref[...])


def _proj_in(x2, mod3, gain, w, S):
    N, D = x2.shape
    NC = w.shape[1]
    tm = _pick(S, 1024, 8)
    tn = _pick(NC, 768)
    return pl.pallas_call(
        _proj_in_kernel,
        grid=(N // tm, NC // tn),
        in_specs=[
            pl.BlockSpec((tm, D), lambda i, j: (i, 0)),
            pl.BlockSpec((1, 1, D), lambda i, j: (i * tm // S, 0, 0)),
            pl.BlockSpec((1, 1, D), lambda i, j: (i * tm // S, 0, 1)),
            pl.BlockSpec((1, D), lambda i, j: (0, 0)),
            pl.BlockSpec((D, tn), lambda i, j: (0, j)),
        ],
        out_specs=pl.BlockSpec((tm, tn), lambda i, j: (i, j)),
        out_shape=jax.ShapeDtypeStruct((N, NC), f32),
        scratch_shapes=[pltpu.VMEM((tm, D), _MXU_DTYPE)],
        compiler_params=_cparams(("arbitrary", "arbitrary")),
        name="proj_in",
    )(x2, mod3, mod3, gain, w)


def _rot_tables(pos_ref, invf_ref, sign_ref):
    ang = pos_ref[...].astype(f32) * invf_ref[...]
    return jnp.cos(ang), jnp.sin(ang) * sign_ref[...]


def _rotate(x, cos_t, sin_t, half, period):
    lane = lax.broadcasted_iota(jnp.int32, x.shape, 1)
    partner = jnp.where((lane % period) < half,
                        pltpu.roll(x, LANES - half, 1),
                        pltpu.roll(x, half, 1))
    return x * cos_t + partner * sin_t


def _rope_consts(half, period, width):
    inv = ROPE_THETA ** (-jnp.arange(half, dtype=f32) / half)
    lane = np.arange(LANES)
    in_rot = ((lane % period) < 2 * half) & (lane < width)
    idx = (lane % period) % half
    invf = jnp.where(jnp.asarray(in_rot), inv[idx], 0.0).reshape(1, LANES).astype(f32)
    sign = np.where(in_rot, np.where((lane % period) < half, -1.0, 1.0), 0.0)
    return invf, jnp.asarray(sign, f32).reshape(1, LANES)


def _mla_up_kernel(cq_ref, ckv_ref, kr_ref, pos_ref, invf_ref, sign_ref, gq_ref, gkv_ref, wq_ref, wkv_ref,
                   q_ref, k_ref, v_ref, *, scale):
    cos_t, sin_t = _rot_tables(pos_ref, invf_ref, sign_ref)
    half = MLA_ROPE // 2
    qn = _rms(cq_ref[...], gq_ref[...]).astype(_MXU_DTYPE)
    q = _mm(qn, wq_ref[...]) * scale
    kvn = _rms(ckv_ref[...], gkv_ref[...]).astype(_MXU_DTYPE)
    kv = _mm(kvn, wkv_ref[...])
    k_rope = _rotate(kr_ref[...], cos_t, sin_t, half, MLA_ROPE).astype(k_ref.dtype)
    lane = lax.broadcasted_iota(jnp.int32, (q.shape[0], LANES), 1)
    H = MLA_HEADS
    for pair in range(H // 2):
        c0 = H * MLA_NOPE + pair * LANES
        r = _rotate(q[:, c0:c0 + LANES], cos_t, sin_t, half, MLA_ROPE)
        r_even = jnp.where(lane < MLA_ROPE, r, 0.0)
        r_odd = jnp.where(lane < MLA_ROPE, pltpu.roll(r, MLA_ROPE, 1), 0.0)
        for h, rr in ((2 * pair, r_even), (2 * pair + 1, r_odd)):
            q_ref[:, h * MLA_QK:h * MLA_QK + LANES] = q[:, h * MLA_NOPE:(h + 1) * MLA_NOPE].astype(q_ref.dtype)
            q_ref[:, h * MLA_QK + LANES:(h + 1) * MLA_QK] = rr.astype(q_ref.dtype)
    for h in range(H):
        k_ref[:, h * MLA_QK:h * MLA_QK + LANES] = kv[:, h * MLA_NOPE:(h + 1) * MLA_NOPE].astype(k_ref.dtype)
        k_ref[:, h * MLA_QK + LANES:(h + 1) * MLA_QK] = k_rope
    v_ref[...] = kv[:, H * MLA_NOPE:].astype(v_ref.dtype)


def _mla_up(p, pos, invf, sign, gq, gkv, wq, wkv, S):
    N = p.shape[0]
    tm = _pick(S, 512, 8)
    H = MLA_HEADS
    kern = functools.partial(_mla_up_kernel, scale=(MLA_NOPE + MLA_ROPE) ** -0.5)
    row = lambda i: (0, 0)
    return pl.pallas_call(
        kern,
        grid=(N // tm,),
        in_specs=[
            pl.BlockSpec((tm, MLA_Q_LORA), lambda i: (i, COL_CQ // MLA_Q_LORA)),
            pl.BlockSpec((tm, MLA_KV_LORA), lambda i: (i, COL_CKV // MLA_KV_LORA)),
            pl.BlockSpec((tm, LANES), lambda i: (i, COL_KR // LANES)),
            pl.BlockSpec((tm, 1), lambda i: (i, 0)),
            pl.BlockSpec((1, LANES), row),
            pl.BlockSpec((1, LANES), row),
            pl.BlockSpec((1, MLA_Q_LORA), row),
            pl.BlockSpec((1, MLA_KV_LORA), row),
            pl.BlockSpec(wq.shape, row),
            pl.BlockSpec(wkv.shape, row),
        ],
        out_specs=[
            pl.BlockSpec((tm, H * MLA_QK), lambda i: (i, 0)),
            pl.BlockSpec((tm, H * MLA_QK), lambda i: (i, 0)),
            pl.BlockSpec((tm, H * MLA_V), lambda i: (i, 0)),
        ],
        out_shape=[
            jax.ShapeDtypeStruct((N, H * MLA_QK), _MXU_DTYPE),
            jax.ShapeDtypeStruct((N, H * MLA_QK), _MXU_DTYPE),
            jax.ShapeDtypeStruct((N, H * MLA_V), _MXU_DTYPE),
        ],
        compiler_params=_cparams(("arbitrary",)),
        name="mla_up",
    )(p, p, p, pos, invf, sign, gq, gkv, wq, wkv)


def _softmax_init(m_ref, acc_ref):
    m_ref[...] = jnp.full(m_ref.shape, NEG, f32)
    acc_ref[...] = jnp.zeros(acc_ref.shape, f32)


def _softmax_step(s_ref, slot, tkc, v, m_ref, a_ref, p_ref, acc_ref, rc):
    rows = m_ref.shape[0]
    nc = tkc // LANES
    for r0 in range(0, rows, rc):
        rs = slice(r0, r0 + rc)
        cols = [s_ref[slot, rs, c * LANES:(c + 1) * LANES] for c in range(nc)]
        m = m_ref[rs]
        m_new = jnp.maximum(m, jnp.max(functools.reduce(jnp.maximum, cols), axis=1, keepdims=True))
        psum = None
        for c in range(nc):
            pc = jnp.exp(cols[c] - m_new)
            p_ref[rs, c * LANES:(c + 1) * LANES] = pc.astype(p_ref.dtype)
            psum = pc if psum is None else psum + pc
        m_ref[rs] = m_new
        alpha_c = jnp.exp(m - m_new)
        a_ref[rs] = alpha_c
        acc_ref[rs, LANES:] = alpha_c * acc_ref[rs, LANES:] + jnp.sum(psum, axis=1, keepdims=True)
    alpha = a_ref[...]
    acc_ref[:, :LANES] = alpha * acc_ref[:, :LANES] + _mm(p_ref[:, :tkc], v)


def _softmax_result(acc_ref):
    return acc_ref[:, :LANES] / acc_ref[:, LANES:]


def _tile_pipeline(n, produce, consume, finish_last=None):
    odd = n % 2

    @pl.when(odd == 1)
    def _():
        produce(0, 0)
        if finish_last is not None:
            pl.when(n == 1)(lambda: finish_last(0))
        consume(0, 0)

    @pl.when(n >= 2)
    def _():
        produce(odd, 0)

        def pair(jj, carry):
            j = odd + 2 * jj
            produce(j + 1, 1)
            consume(j, 0)
            produce(j + 2, 0)
            consume(j + 1, 1)
            return carry

        lax.fori_loop(0, n // 2 - 1, pair, 0)
        produce(n - 1, 1)
        if finish_last is not None:
            finish_last(1)
        consume(n - 2, 0)
        consume(n - 1, 1)


def _flash_kernel(*refs, tq, tk, rc, scale, decay):
    if decay:
        q_ref, k_ref, v_ref, cq_ref, ck_ref, o_ref, s_ref, p_ref, m_ref, a_ref, acc_ref = refs
    else:
        q_ref, k_ref, v_ref, o_ref, s_ref, p_ref, m_ref, a_ref, acc_ref = refs
    i = pl.program_id(2)
    q = q_ref[...]
    if scale != 1.0:
        q = q.astype(f32) * scale
    q = q.astype(_MXU_DTYPE)
    if decay:
        cq = cq_ref[...]

    def produce(kt, slot):
        k0 = pl.multiple_of(kt * tk, tk)
        s = _mm_nt(q, k_ref[pl.ds(k0, tk), :].astype(_MXU_DTYPE))
        if decay:
            ck = ck_ref[0, 0, :, pl.ds(k0, tk)]
            for c in range(tk // LANES):
                cs = slice(c * LANES, (c + 1) * LANES)
                s_ref[slot, :, cs] = (s[:, cs] - ck[:, cs]) + cq
        else:
            s_ref[slot] = s

    def mask_diagonal(slot):
        col = lax.broadcasted_iota(jnp.int32, (1, tk), 1)
        for r0 in range(0, tq, rc):
            row = r0 + lax.broadcasted_iota(jnp.int32, (rc, 1), 0)
            s_ref[slot, r0:r0 + rc, :] = jnp.where(col <= row, s_ref[slot, r0:r0 + rc, :], NEG)

    def consume(kt, slot):
        k0 = pl.multiple_of(kt * tk, tk)
        _softmax_step(s_ref, slot, tk, v_ref[pl.ds(k0, tk), :].astype(_MXU_DTYPE),
                      m_ref, a_ref, p_ref, acc_ref, rc)

    _softmax_init(m_ref, acc_ref)
    _tile_pipeline(i + 1, produce, consume, mask_diagonal)
    o_ref[...] = _softmax_result(acc_ref).astype(o_ref.dtype)


def _flash(q, k, v, *, B, S, H, dk, dv, qcol, kcol, vcol, scale, cum_col=None, cum_row=None):
    N = B * S
    tq = _pick(S, 512, 8)
    tk = _pick(S, 512, LANES)
    rc = _pick(tq, 64, 8)
    nq = S // tq
    assert tq == tk and dv == LANES
    decay = cum_col is not None
    in_specs = [
        pl.BlockSpec((tq, dk), lambda b, h, i: (b * nq + i, qcol + h)),
        pl.BlockSpec((S, dk), lambda b, h, i: (b, kcol + h)),
        pl.BlockSpec((S, dv), lambda b, h, i: (b, vcol + h)),
    ]
    args = [q, k, v]
    if decay:
        in_specs += [
            pl.BlockSpec((tq, LANES), lambda b, h, i: (b * nq + i, h)),
            pl.BlockSpec((1, 1, 1, S), lambda b, h, i: (b, h, 0, 0)),
        ]
        args += [cum_col, cum_row]
    kern = functools.partial(_flash_kernel, tq=tq, tk=tk, rc=rc, scale=scale, decay=decay)
    return pl.pallas_call(
        kern,
        grid=(B, H, nq),
        in_specs=in_specs,
        out_specs=pl.BlockSpec((tq, dv), lambda b, h, i: (b * nq + i, h)),
        out_shape=jax.ShapeDtypeStruct((N, H * dv), f32),
        scratch_shapes=[
            pltpu.VMEM((2, tq, tk), f32),
            pltpu.VMEM((tq, tk), _MXU_DTYPE),
            pltpu.VMEM((tq, LANES), f32),
            pltpu.VMEM((tq, LANES), f32),
            pltpu.VMEM((tq, 2 * LANES), f32),
        ],
        compiler_params=_cparams(("arbitrary", "arbitrary", "arbitrary")),
        name="flash_fox" if decay else "flash_mla",
    )(*args)


def _fox_prep_kernel(ff_ref, b_ref, o_ref, rep_ref, *, chunk):
    S = ff_ref.shape[0]
    r = lax.broadcasted_iota(jnp.int32, (chunk, chunk), 0)
    c = lax.broadcasted_iota(jnp.int32, (chunk, chunk), 1)
    tri = (c <= r).astype(_MXU_DTYPE)

    def body(j, carry):
        r0 = pl.multiple_of(j * chunk, chunk)
        x = ff_ref[pl.ds(r0, chunk), :] + b_ref[...]
        lf = jnp.minimum(x, 0.0) - jnp.log(1.0 + jnp.exp(-jnp.abs(x)))
        hi, mid, lo = _split3(lf)
        cum = (_mm(tri, hi) + _mm(tri, mid)) + _mm(tri, lo) + carry
        o_ref[pl.ds(r0, chunk), :] = cum
        for h in range(FOX_HEADS):
            rep_ref[pl.ds(r0, chunk), h * LANES:(h + 1) * LANES] = jnp.broadcast_to(cum[:, h:h + 1], (chunk, LANES))
        return cum[chunk - 1:chunk, :]

    lax.fori_loop(0, S // chunk, body, jnp.zeros((1, LANES), f32))


def _fox_prep(p, b_pad, B, S):
    N = B * S
    kern = functools.partial(_fox_prep_kernel, chunk=_pick(S, 128, 8))
    return pl.pallas_call(
        kern,
        grid=(B,),
        in_specs=[
            pl.BlockSpec((S, LANES), lambda b: (b, COL_FF // LANES)),
            pl.BlockSpec((1, LANES), lambda b: (0, 0)),
        ],
        out_specs=[pl.BlockSpec((S, LANES), lambda b: (b, 0)),
                   pl.BlockSpec((S, FOX_HEADS * LANES), lambda b: (b, 0))],
        out_shape=[jax.ShapeDtypeStruct((N, LANES), f32),
                   jax.ShapeDtypeStruct((N, FOX_HEADS * LANES), f32)],
        compiler_params=_cparams(("arbitrary",)),
        name="fox_prep",
    )(p, b_pad)


def _nsa_prep_kernel(q_ref, kc_ref, vc_ref, ks_ref, vs_ref, kw_ref, vw_ref, pos_ref, invf_ref, sign_ref, blk_ref,
                     qo_ref, cmp_ref, kso_ref, vso_ref, kwo_ref, vwo_ref, kc_scr, *, scale):
    cos_t, sin_t = _rot_tables(pos_ref, invf_ref, sign_ref)
    half = PARTIAL_ROT // 2
    rot = lambda x: _rotate(x, cos_t, sin_t, half, LANES)
    for h in range(NSA_HEADS):
        sl = slice(h * HEAD_DIM, (h + 1) * HEAD_DIM)
        qo_ref[:, sl] = (rot(q_ref[:, sl]) * scale).astype(qo_ref.dtype)
    nrow = cmp_ref.shape[2]
    for g in range(NSA_KV_GROUPS):
        sl = slice(g * HEAD_DIM, (g + 1) * HEAD_DIM)
        kso_ref[:, 2 * g * HEAD_DIM:(2 * g + 1) * HEAD_DIM] = rot(ks_ref[:, sl]).astype(kso_ref.dtype)
        kso_ref[:, (2 * g + 1) * HEAD_DIM:(2 * g + 2) * HEAD_DIM] = blk_ref[...]
        kwo_ref[:, sl] = rot(kw_ref[:, sl]).astype(kwo_ref.dtype)
        kc_scr[0] = rot(kc_ref[:, sl])
        kc_scr[1] = vc_ref[:, sl]
        for t in range(NSA_CMP_STRIDE):
            cs = slice(t * HEAD_DIM, (t + 1) * HEAD_DIM)
            for kv in range(2):
                cmp_ref[kv, g, :, cs] = kc_scr[kv, pl.ds(t, nrow, stride=NSA_CMP_STRIDE), :].astype(cmp_ref.dtype)
    vso_ref[...] = vs_ref[...].astype(vso_ref.dtype)
    vwo_ref[...] = vw_ref[...].astype(vwo_ref.dtype)


def _nsa_prep(p, pos, invf, sign, expand, S):
    N = p.shape[0]
    tm = _pick(S, 512, 8)
    nt = S // tm
    G = NSA_KV_GROUPS
    cw = NSA_CMP_STRIDE * HEAD_DIM
    kv = lambda col: pl.BlockSpec((tm, NSA_KV), lambda i: (i, col // NSA_KV))
    row = lambda i: (0, 0)
    kern = functools.partial(_nsa_prep_kernel, scale=HEAD_DIM ** -0.5)
    return pl.pallas_call(
        kern,
        grid=(N // tm,),
        in_specs=[
            pl.BlockSpec((tm, W_NSA), lambda i: (i, COL_NQ // W_NSA)),
            kv(COL_NKC), kv(COL_NVC), kv(COL_NKS), kv(COL_NVS), kv(COL_NKW), kv(COL_NVW),
            pl.BlockSpec((tm, 1), lambda i: (i, 0)),
            pl.BlockSpec((1, LANES), row),
            pl.BlockSpec((1, LANES), row),
            pl.BlockSpec((tm, LANES), lambda i: (i % nt, 0)),
        ],
        out_specs=[
            pl.BlockSpec((tm, W_NSA), lambda i: (i, 0)),
            pl.BlockSpec((2, G, tm // NSA_CMP_STRIDE, cw), lambda i: (0, 0, i, 0)),
            pl.BlockSpec((tm, 2 * NSA_KV), lambda i: (i, 0)),
            pl.BlockSpec((tm, NSA_KV), lambda i: (i, 0)),
            pl.BlockSpec((tm, NSA_KV), lambda i: (i, 0)),
            pl.BlockSpec((tm, NSA_KV), lambda i: (i, 0)),
        ],
        out_shape=[
            jax.ShapeDtypeStruct((N, W_NSA), _MXU_DTYPE),
            jax.ShapeDtypeStruct((2, G, N // NSA_CMP_STRIDE, cw), _MXU_DTYPE),
            jax.ShapeDtypeStruct((N, 2 * NSA_KV), _MXU_DTYPE),
            jax.ShapeDtypeStruct((N, NSA_KV), _MXU_DTYPE),
            jax.ShapeDtypeStruct((N, NSA_KV), _MXU_DTYPE),
            jax.ShapeDtypeStruct((N, NSA_KV), _MXU_DTYPE),
        ],
        scratch_shapes=[pltpu.VMEM((2, tm, HEAD_DIM), f32)],
        compiler_params=_cparams(("arbitrary",)),
        name="nsa_prep",
    )(p, p, p, p, p, p, p, pos, invf, sign, expand)


def _compress_kernel(c_ref, pos_ref, w1_ref, w2_ref, o_ref):
    x = c_ref[0, 0, 0]
    half = x.shape[1]
    nrow = x.shape[0]
    a = _mm(x, w1_ref[0, :half, :])
    b = _mm(x, w1_ref[0, half:, :])
    posb = _mm(pos_ref[0].astype(_MXU_DTYPE), w1_ref[0])[0:1, :]
    hid = a + pltpu.roll(b, nrow - 1, 0) + posb
    c0 = math.sqrt(2.0 / math.pi)
    act = 0.5 * hid * (1.0 + jnp.tanh(c0 * (hid + 0.044715 * (hid * hid * hid))))
    o_ref[0, 0, 0] = _mm(act.astype(_MXU_DTYPE), w2_ref[0])


def _compress(cmp_in, pos_flat, w1, w2, B, S):
    G = NSA_KV_GROUPS
    nch = S // NSA_CMP_STRIDE
    width = NSA_CMP_STRIDE * HEAD_DIM
    x = cmp_in.reshape(2, G, B, nch, width)
    return pl.pallas_call(
        _compress_kernel,
        grid=(2, G, B),
        in_specs=[
            pl.BlockSpec((1, 1, 1, nch, width), lambda t, g, b: (t, g, b, 0, 0)),
            pl.BlockSpec((1, 8, 2 * width), lambda t, g, b: (t, 0, 0)),
            pl.BlockSpec((1, 2 * width, NSA_CMP_HIDDEN), lambda t, g, b: (t, 0, 0)),
            pl.BlockSpec((1, NSA_CMP_HIDDEN, HEAD_DIM), lambda t, g, b: (t, 0, 0)),
        ],
        out_specs=pl.BlockSpec((1, 1, 1, nch, HEAD_DIM), lambda t, g, b: (t, g, b, 0, 0)),
        out_shape=jax.ShapeDtypeStruct((2, G, B, nch, HEAD_DIM), f32),
        compiler_params=_cparams(("arbitrary", "arbitrary", "arbitrary")),
        name="nsa_compress",
    )(x, pos_flat, w1, w2)


def _nsa_attn_kernel(q_ref, kc_ref, vc_ref, ks_ref, vs_ref, kw_ref, vw_ref, g_ref, ovl_ref, o_ref,
                     s_ref, p_ref, m_ref, a_ref, acc_ref, out_scr, psum_ref,
                     *, tq, tks, tkw, rc, n_slc, top_k):
    i = pl.program_id(2)
    Hg = NSA_GQA
    D = HEAD_DIM
    q4 = q_ref[...]
    qs = jnp.concatenate([q4[:, h * D:(h + 1) * D] for h in range(Hg)], axis=0)
    t = i * tq + lax.broadcasted_iota(jnp.int32, (tq, 1), 0)
    gate = jax.nn.sigmoid(g_ref[...])
    head_rows = lambda h: slice(h * tq, (h + 1) * tq)

    kc = kc_ref[0, 0, 0].astype(_MXU_DTYPE)
    vc = vc_ref[0, 0, 0].astype(_MXU_DTYPE)
    ncmp = kc.shape[0]
    cend = lax.broadcasted_iota(jnp.int32, (1, ncmp), 1) * NSA_CMP_STRIDE + (NSA_CMP_LEN - 1)
    s_ref[0, :, :ncmp] = _mm_nt(qs, kc)
    cq = LANES
    for h in range(Hg):
        for q0 in range(0, tq, cq):
            rs = slice(h * tq + q0, h * tq + q0 + cq)
            valid = cend <= t[q0:q0 + cq]
            s = jnp.where(valid, s_ref[0, rs, :ncmp], NEG)
            e = jnp.where(valid, jnp.exp(s - jnp.max(s, axis=1, keepdims=True)), 0.0)
            l = jnp.sum(e, axis=1, keepdims=True)
            p = e / jnp.where(l > 0.0, l, 1.0)
            p_ref[rs, :ncmp] = p.astype(p_ref.dtype)
            if h == 0:
                psum_ref[q0:q0 + cq] = p
            else:
                psum_ref[q0:q0 + cq] += p
    o_cmp = _mm(p_ref[:, :ncmp], vc)
    for h in range(Hg):
        out_scr[head_rows(h)] = gate[:, 3 * h:3 * h + 1] * o_cmp[head_rows(h)]

    hi, mid, lo = _split3(psum_ref[...])
    ovl = ovl_ref[...]
    imp = (_mm(hi, ovl) + _mm(mid, ovl)) + _mm(lo, ovl)
    nb = -(-n_slc // 8) * 8
    x = imp.T[:nb]
    jr = lax.broadcasted_iota(jnp.int32, (nb, 1), 0)
    cur_l = (i * tq + lax.broadcasted_iota(jnp.int32, (1, tq), 1)) // NSA_SLC_BLOCK
    forced = (jr == 0) | (jr == cur_l) | (jr == cur_l - 1)
    x = jnp.where(forced, BIG, jnp.where(jr > cur_l, NEG, x))
    rank = jnp.zeros((nb, tq), f32)
    for kk in range(n_slc):
        ck = x[kk:kk + 1, :]
        before = (ck > x) | ((ck == x) & (jr > kk))
        rank = rank + before.astype(f32)
    unsel_t = jnp.where((rank < float(top_k)) & (jr <= cur_l), 0.0, NEG)
    if nb < LANES:
        unsel_t = jnp.concatenate([unsel_t, jnp.zeros((LANES - nb, tq), f32)], axis=0)
    unsel = unsel_t.T.astype(_MXU_DTYPE)
    q_aug = jnp.concatenate([qs, jnp.concatenate([unsel] * Hg, axis=0)], axis=1)

    def store_masked(slot, s, allowed, width):
        bias = jnp.where(allowed, 0.0, NEG)
        for h in range(Hg):
            s_ref[slot, head_rows(h), :width] = s[head_rows(h)] + bias

    def add_gated(c):
        o = _softmax_result(acc_ref)
        for h in range(Hg):
            out_scr[head_rows(h)] += gate[:, 3 * h + c:3 * h + c + 1] * o[head_rows(h)]

    n_ts = ((i + 1) * tq + tks - 1) // tks

    def slc_produce(kt, slot):
        k0 = pl.multiple_of(kt * tks, tks)
        s_ref[slot] = _mm_nt(q_aug, ks_ref[pl.ds(k0, tks), :])

    def slc_mask_last(slot):
        col = (n_ts - 1) * tks + lax.broadcasted_iota(jnp.int32, (1, tks), 1)
        for r0 in range(0, Hg * tq, rc):
            q0 = r0 % tq
            s_ref[slot, r0:r0 + rc, :] = jnp.where(col <= t[q0:q0 + rc], s_ref[slot, r0:r0 + rc, :], NEG)

    def slc_consume(kt, slot):
        k0 = pl.multiple_of(kt * tks, tks)
        _softmax_step(s_ref, slot, tks, vs_ref[pl.ds(k0, tks), :], m_ref, a_ref, p_ref, acc_ref, rc)

    _softmax_init(m_ref, acc_ref)
    _tile_pipeline(n_ts, slc_produce, slc_consume, slc_mask_last)
    add_gated(1)

    hi_t = (i * tq + tq - 1) // tkw
    lo_t = jnp.maximum(i * tq - (NSA_WINDOW - 1), 0) // tkw

    def win_produce(j, slot):
        k0 = pl.multiple_of((hi_t - j) * tkw, tkw)
        col = k0 + lax.broadcasted_iota(jnp.int32, (1, tkw), 1)
        store_masked(slot, _mm_nt(qs, kw_ref[pl.ds(k0, tkw), :]), (col <= t) & (t - col < NSA_WINDOW), tkw)

    def win_consume(j, slot):
        k0 = pl.multiple_of((hi_t - j) * tkw, tkw)
        _softmax_step(s_ref, slot, tkw, vw_ref[pl.ds(k0, tkw), :], m_ref, a_ref, p_ref, acc_ref,
                      rc * (tks // tkw))

    _softmax_init(m_ref, acc_ref)
    _tile_pipeline(hi_t - lo_t + 1, win_produce, win_consume)
    add_gated(2)
    for h in range(Hg):
        o_ref[:, h * D:(h + 1) * D] = out_scr[head_rows(h)].astype(o_ref.dtype)


def _nsa_attn(qn, cmp_kv, ks, vs, kw, vw, p, ovl, B, S):
    N = B * S
    G = NSA_KV_GROUPS
    D = HEAD_DIM
    tq = _pick(S, 256, LANES)
    tks = _pick(S, 512, LANES)
    tkw = _pick(S, 256, LANES)
    rc = _pick(tq, 64, 8)
    nq = S // tq
    nch = cmp_kv.shape[3]
    n_slc = S // NSA_SLC_BLOCK
    assert n_slc <= LANES and nch <= tks and tkw <= tks
    kern = functools.partial(_nsa_attn_kernel, tq=tq, tks=tks, tkw=tkw, rc=rc, n_slc=n_slc,
                             top_k=min(NSA_TOPK, n_slc))
    R = NSA_GQA * tq
    tk = tks
    seq = lambda: pl.BlockSpec((S, D), lambda b, g, i: (b, g))
    return pl.pallas_call(
        kern,
        grid=(B, G, nq),
        in_specs=[
            pl.BlockSpec((tq, NSA_GQA * D), lambda b, g, i: (b * nq + i, g)),
            pl.BlockSpec((1, 1, 1, nch, D), lambda b, g, i: (0, g, b, 0, 0)),
            pl.BlockSpec((1, 1, 1, nch, D), lambda b, g, i: (1, g, b, 0, 0)),
            pl.BlockSpec((S, 2 * D), lambda b, g, i: (b, g)),
            seq(), seq(), seq(),
            pl.BlockSpec((tq, LANES), lambda b, g, i: (b * nq + i, COL_NG // LANES + g)),
            pl.BlockSpec(ovl.shape, lambda b, g, i: (0, 0)),
        ],
        out_specs=pl.BlockSpec((tq, NSA_GQA * D), lambda b, g, i: (b * nq + i, g)),
        out_shape=jax.ShapeDtypeStruct((N, W_NSA), f32),
        scratch_shapes=[
            pltpu.VMEM((2, R, tk), f32),
            pltpu.VMEM((R, tk), _MXU_DTYPE),
            pltpu.VMEM((R, LANES), f32),
            pltpu.VMEM((R, LANES), f32),
            pltpu.VMEM((R, 2 * D), f32),
            pltpu.VMEM((R, D), f32),
            pltpu.VMEM((tq, nch), f32),
        ],
        compiler_params=_cparams(("arbitrary", "arbitrary", "arbitrary")),
        name="nsa_attn",
    )(qn, cmp_kv, cmp_kv, ks, vs, kw, vw, p, ovl)


def _nsa_consts(S):
    nch = S // NSA_CMP_STRIDE
    n_slc = S // NSA_SLC_BLOCK
    c_start = np.arange(nch) * NSA_CMP_STRIDE
    s_start = np.arange(n_slc) * NSA_SLC_BLOCK
    ov = (np.minimum(c_start[:, None] + NSA_CMP_LEN, s_start[None, :] + NSA_SLC_BLOCK)
          - np.maximum(c_start[:, None], s_start[None, :]))
    ovl = np.zeros((nch, LANES), np.float32)
    ovl[:, :n_slc] = np.clip(ov, 0, None) / NSA_CMP_LEN
    expand = np.zeros((S, LANES), np.float32)
    expand[np.arange(S), np.arange(S) // NSA_SLC_BLOCK] = 1.0
    return jnp.asarray(ovl, _MXU_DTYPE), jnp.asarray(expand, _MXU_DTYPE)


def _out_proj_kernel(om_ref, of_ref, on_ref, gm_ref, gf_ref, gn_ref, w_ref, x_ref, gt_ref, o_ref, h_ref):
    @pl.when(pl.program_id(1) == 0)
    def _():
        h_ref[:, :W_MLA] = _rms(om_ref[...], gm_ref[...]).astype(h_ref.dtype)
        h_ref[:, W_MLA:W_MLA + W_FOX] = _rms(of_ref[...], gf_ref[...]).astype(h_ref.dtype)
        h_ref[:, W_MLA + W_FOX:] = _rms(on_ref[...], gn_ref[...]).astype(h_ref.dtype)

    o_ref[...] = x_ref[...] + gt_ref[0] * _mm(h_ref[...], w_ref[...])


def _out_proj(o_mla, o_fox, o_nsa, gm, gf, gn, w, x2, mod3, S):
    N, D = x2.shape
    DM = w.shape[0]
    tm = _pick(S, 512, 8)
    tn = _pick(D, 2048)
    nj = D // tn
    row = lambda i, j: (0, 0)
    return pl.pallas_call(
        _out_proj_kernel,
        grid=(N // tm, nj),
        in_specs=[
            pl.BlockSpec((tm, W_MLA), lambda i, j: (i, 0)),
            pl.BlockSpec((tm, W_FOX), lambda i, j: (i, 0)),
            pl.BlockSpec((tm, W_NSA), lambda i, j: (i, 0)),
            pl.BlockSpec((1, W_MLA), row),
            pl.BlockSpec((1, W_FOX), row),
            pl.BlockSpec((1, W_NSA), row),
            pl.BlockSpec((DM, tn), lambda i, j: (0, j)),
            pl.BlockSpec((tm, tn), lambda i, j: (i, j)),
            pl.BlockSpec((1, 1, tn), lambda i, j: (i * tm // S, 0, 2 * nj + j)),
        ],
        out_specs=pl.BlockSpec((tm, tn), lambda i, j: (i, j)),
        out_shape=jax.ShapeDtypeStruct((N, D), f32),
        scratch_shapes=[pltpu.VMEM((tm, DM), _MXU_DTYPE)],
        compiler_params=_cparams(("arbitrary", "arbitrary")),
        name="out_proj",
    )(o_mla, o_fox, o_nsa, gm, gf, gn, w, x2, mod3)


def _mlp_kernel(x_ref, sh_ref, sc_ref, gt_ref, g_ref, w1_ref, w2_ref, fg_ref, o_ref, h_ref, acc_ref, *, final):
    f = pl.program_id(1)

    @pl.when(f == 0)
    def _():
        h = _rms(x_ref[...], g_ref[...]) * (1.0 + sc_ref[0]) + sh_ref[0]
        h_ref[...] = h.astype(h_ref.dtype)
        acc_ref[...] = jnp.zeros_like(acc_ref)

    u = jnp.maximum(_mm(h_ref[...], w1_ref[...]), 0.0)
    acc_ref[...] += _mm((u * u).astype(_MXU_DTYPE), w2_ref[...])

    @pl.when(f == pl.num_programs(1) - 1)
    def _():
        y = x_ref[...] + gt_ref[0] * acc_ref[...]
        if final:
            y = _rms(y, fg_ref[...])
        o_ref[...] = y


def _mlp(x2, mod3, gain, w1, w2, final_gain, S, final):
    N, D = x2.shape
    F = w1.shape[1]
    tm = _pick(S, 512, 8)
    tf = _pick(F, 1024)
    row = lambda i, f: (0, 0)
    mod_blk = lambda c: pl.BlockSpec((1, 1, D), lambda i, f: (i * tm // S, 0, c))
    return pl.pallas_call(
        functools.partial(_mlp_kernel, final=final),
        grid=(N // tm, F // tf),
        in_specs=[
            pl.BlockSpec((tm, D), lambda i, f: (i, 0)),
            mod_blk(3), mod_blk(4), mod_blk(5),
            pl.BlockSpec((1, D), row),
            pl.BlockSpec((D, tf), lambda i, f: (0, f)),
            pl.BlockSpec((tf, D), lambda i, f: (f, 0)),
            pl.BlockSpec((1, D), row),
        ],
        out_specs=pl.BlockSpec((tm, D), lambda i, f: (i, 0)),
        out_shape=jax.ShapeDtypeStruct((N, D), f32),
        scratch_shapes=[pltpu.VMEM((tm, D), _MXU_DTYPE), pltpu.VMEM((tm, D), f32)],
        compiler_params=_cparams(("arbitrary", "arbitrary")),
        name="mlp",
    )(x2, mod3, mod3, mod3, gain, w1, w2, final_gain)


def _w_in_segments():
    widths = (MLA_Q_LORA, MLA_KV_LORA, MLA_ROPE, W_FOX, W_FOX, W_FOX, FOX_HEADS,
              W_NSA, NSA_KV, NSA_KV, NSA_KV, NSA_KV, NSA_KV, NSA_KV, 3 * NSA_HEADS)
    src = np.cumsum((0,) + widths)
    (cq, ckv, kr, fq, fk, fv, ff, nq, nkc, nvc, nks, nvs, nkw, nvw, ng) = [int(s) for s in src[:-1]]
    gw = 3 * NSA_GQA
    return [(COL_NQ, nq, W_NSA), (COL_CQ, cq, MLA_Q_LORA), (COL_FQ, fq, W_FOX), (COL_FK, fk, W_FOX),
            (COL_FV, fv, W_FOX), (COL_CKV, ckv, MLA_KV_LORA), (COL_NKC, nkc, NSA_KV), (COL_NVC, nvc, NSA_KV),
            (COL_NKS, nks, NSA_KV), (COL_NVS, nvs, NSA_KV), (COL_NKW, nkw, NSA_KV), (COL_NVW, nvw, NSA_KV),
            (COL_KR, kr, MLA_ROPE), (COL_FF, ff, FOX_HEADS), (COL_NG, ng, gw), (COL_NG + LANES, ng + gw, gw)]


def _w_in_regroup_kernel(w_ref, o_ref):
    o_ref[...] = jnp.zeros(o_ref.shape, o_ref.dtype)
    for dst, src, width in _w_in_segments():
        o_ref[:, dst:dst + width] = w_ref[0, :, src:src + width].astype(o_ref.dtype)


def _prep_w_in(w_in, l):
    _, D, C = w_in.shape
    tr = _pick(D, 256, 8)
    return pl.pallas_call(
        _w_in_regroup_kernel,
        grid=(D // tr,),
        in_specs=[pl.BlockSpec((1, tr, C), lambda i: (l, i, 0))],
        out_specs=pl.BlockSpec((tr, N_COLS), lambda i: (i, 0)),
        out_shape=jax.ShapeDtypeStruct((D, N_COLS), _MXU_DTYPE),
        compiler_params=_cparams(("arbitrary",)),
        name="w_in_regroup",
    )(w_in)


def _prep_mla_w(w_uq, w_ukv):
    H = MLA_HEADS
    qd = MLA_NOPE + MLA_ROPE
    q_nope = np.concatenate([np.arange(h * qd, h * qd + MLA_NOPE) for h in range(H)])
    q_rope = np.concatenate([np.arange(h * qd + MLA_NOPE, (h + 1) * qd) for h in range(H)])
    kd = MLA_NOPE + MLA_V
    k_nope = np.concatenate([np.arange(h * kd, h * kd + MLA_NOPE) for h in range(H)])
    v_cols = np.concatenate([np.arange(h * kd + MLA_NOPE, (h + 1) * kd) for h in range(H)])
    wq = jnp.take(w_uq, jnp.asarray(np.concatenate([q_nope, q_rope])), axis=1).astype(_MXU_DTYPE)
    wkv = jnp.take(w_ukv, jnp.asarray(np.concatenate([k_nope, v_cols])), axis=1).astype(_MXU_DTYPE)
    return wq, wkv


def _pad_lanes(v):
    return jnp.zeros((1, LANES), f32).at[0, :v.shape[0]].set(v.astype(f32))


def kernel(x, c, positions, ada_w, ada_b, norm_mix, norm_mlp, w_in, fox_b_f, mla_q_norm, mla_w_uq, mla_kv_norm,
           mla_w_ukv, nsa_pos_k, nsa_pos_v, nsa_cmp_k_w1, nsa_cmp_k_w2, nsa_cmp_v_w1, nsa_cmp_v_w2, out_norm,
           w_out, mlp_w1, mlp_w2, final_norm):
    B, S, D = x.shape
    L = ada_w.shape[0]
    N = B * S
    assert S % LANES == 0 and D % LANES == 0 and B <= 8

    x2 = x.reshape(N, D)
    pos = positions.reshape(N, 1).astype(jnp.int32)
    c_pad = jnp.zeros((8, D), f32).at[:B].set(c)
    mod = _ada_mod(c_pad, ada_w, ada_b)

    invf_mla, sign_mla = _rope_consts(MLA_ROPE // 2, MLA_ROPE, LANES)
    invf_nsa, sign_nsa = _rope_consts(PARTIAL_ROT // 2, LANES, LANES)
    ovl, expand = _nsa_consts(S)
    row = lambda v: v.reshape(1, -1)

    for l in range(L):
        mod3 = mod[l, :B].reshape(B, 1, 6 * D)
        p = _proj_in(x2, mod3, row(norm_mix[l]), _prep_w_in(w_in, l), S)

        wq, wkv = _prep_mla_w(mla_w_uq[l], mla_w_ukv[l])
        q_m, k_m, v_m = _mla_up(p, pos, invf_mla, sign_mla, row(mla_q_norm[l]), row(mla_kv_norm[l]), wq, wkv, S)
        o_mla = _flash(q_m, k_m, v_m, B=B, S=S, H=MLA_HEADS, dk=MLA_QK, dv=MLA_V, qcol=0, kcol=0, vcol=0,
                       scale=1.0)

        cum, cum_rep = _fox_prep(p, _pad_lanes(fox_b_f[l]), B, S)
        cum_row = cum[:, :FOX_HEADS].reshape(B, S, FOX_HEADS).transpose(0, 2, 1)[:, :, None, :]
        o_fox = _flash(p, p, p, B=B, S=S, H=FOX_HEADS, dk=HEAD_DIM, dv=HEAD_DIM,
                       qcol=COL_FQ // HEAD_DIM, kcol=COL_FK // HEAD_DIM, vcol=COL_FV // HEAD_DIM,
                       scale=HEAD_DIM ** -0.5, cum_col=cum_rep, cum_row=cum_row)

        qn, cmp_in, ks, vs, kw, vw = _nsa_prep(p, pos, invf_nsa, sign_nsa, expand, S)
        flat = lambda a: jnp.broadcast_to(a.reshape(1, -1), (8, a.size))
        cmp_kv = _compress(
            cmp_in,
            jnp.stack([flat(nsa_pos_k[l]), flat(nsa_pos_v[l])]).astype(f32),
            jnp.stack([nsa_cmp_k_w1[l], nsa_cmp_v_w1[l]]).astype(_MXU_DTYPE),
            jnp.stack([nsa_cmp_k_w2[l], nsa_cmp_v_w2[l]]).astype(_MXU_DTYPE), B, S)
        o_nsa = _nsa_attn(qn, cmp_kv, ks, vs, kw, vw, p, ovl, B, S)

        gn = out_norm[l]
        x2 = _out_proj(o_mla, o_fox, o_nsa, row(gn[:W_MLA]), row(gn[W_MLA:W_MLA + W_FOX]),
                       row(gn[W_MLA + W_FOX:]), w_out[l].astype(_MXU_DTYPE), x2, mod3, S)
        x2 = _mlp(x2, mod3, row(norm_mlp[l]), mlp_w1[l].astype(_MXU_DTYPE), mlp_w2[l].astype(_MXU_DTYPE),
                  row(final_norm), S, final=(l == L - 1))
    return x2.reshape(B, S, D)
```

```python
import functools
import math

import numpy as np
import jax
import jax.numpy as jnp
from jax import lax
from jax.experimental import pallas as pl
from jax.experimental.pallas import tpu as pltpu

f32 = jnp.float32
_MXU_DTYPE = jnp.bfloat16

HEAD_DIM = 128
MLA_HEADS = 4
MLA_Q_LORA = 512
MLA_KV_LORA = 256
MLA_NOPE = 128
MLA_ROPE = 64
MLA_V = 128
FOX_HEADS = 4
NSA_HEADS = 8
NSA_KV_GROUPS = 2
NSA_GQA = NSA_HEADS // NSA_KV_GROUPS
NSA_CMP_LEN = 32
NSA_CMP_STRIDE = 16
NSA_CMP_HIDDEN = 512
NSA_SLC_BLOCK = 64
NSA_TOPK = 16
NSA_WINDOW = 512
W_MLA = MLA_HEADS * MLA_V
W_FOX = FOX_HEADS * HEAD_DIM
W_NSA = NSA_HEADS * HEAD_DIM
NSA_KV = NSA_KV_GROUPS * HEAD_DIM
ROPE_THETA = 500000.0
PARTIAL_ROT = HEAD_DIM // 4
EPS = 1e-6
NEG = -1e30
BIG = 1e30

LANES = 128
VMEM_LIMIT = 56 * 1024 * 1024

MLA_QK = 2 * LANES
COL_NQ = 0
COL_CQ = 1024
COL_FQ = 1536
COL_FK = 2048
COL_FV = 2560
COL_CKV = 3072
COL_NKC = 3328
COL_NVC = 3584
COL_NKS = 3840
COL_NVS = 4096
COL_NKW = 4352
COL_NVW = 4608
COL_KR = 4864
COL_FF = 4992
COL_NG = 5120
N_COLS = 5376


def _pick(n, cap, mult=LANES):
    if n <= cap:
        return n
    t = (cap // mult) * mult
    while t >= mult:
        if n % t == 0:
            return t
        t -= mult
    raise ValueError(f"no tile for {n} under {cap}")


def _mm(a, b):
    return jnp.dot(a, b, preferred_element_type=f32)


def _mm_nt(a, b):
    return lax.dot_general(a, b, (((1,), (1,)), ((), ())), preferred_element_type=f32)


def _split3(a):
    hi = a.astype(_MXU_DTYPE)
    r1 = a - hi.astype(f32)
    mid = r1.astype(_MXU_DTYPE)
    lo = (r1 - mid.astype(f32)).astype(_MXU_DTYPE)
    return hi, mid, lo


def _rms(x, gain):
    return x * lax.rsqrt(jnp.mean(x * x, axis=-1, keepdims=True) + EPS) * gain


def _cparams(sem):
    return pltpu.CompilerParams(dimension_semantics=sem, vmem_limit_bytes=VMEM_LIMIT)


def _ada_kernel(c_ref, w_ref, b_ref, o_ref):
    c = c_ref[...]
    ca = c * jax.nn.sigmoid(c)
    w = w_ref[0]
    c_hi, c_mid, _ = _split3(ca)
    w_hi, w_mid, _ = _split3(w)
    acc = _mm(c_hi, w_hi) + (_mm(c_mid, w_hi) + _mm(c_hi, w_mid))
    o_ref[0] = acc + b_ref[0]


def _ada_mod(c_pad, ada_w, ada_b):
    L, D, D6 = ada_w.shape
    tn = _pick(D6, 1024)
    return pl.pallas_call(
        _ada_kernel,
        grid=(L, D6 // tn),
        in_specs=[
            pl.BlockSpec((8, D), lambda l, j: (0, 0)),
            pl.BlockSpec((1, D, tn), lambda l, j: (l, 0, j)),
            pl.BlockSpec((1, 1, tn), lambda l, j: (l, 0, j)),
        ],
        out_specs=pl.BlockSpec((1, 8, tn), lambda l, j: (l, 0, j)),
        out_shape=jax.ShapeDtypeStruct((L, 8, D6), f32),
        compiler_params=_cparams(("arbitrary", "arbitrary")),
        name="ada_mod",
    )(c_pad, ada_w, ada_b.reshape(L, 1, D6))


def _proj_in_kernel(x_ref, sh_ref, sc_ref, g_ref, w_ref, o_ref, h_ref):
    @pl.when(pl.program_id(1) == 0)
    def _():
        h = _rms(x_ref[...], g_ref[...]) * (1.0 + sc_ref[0]) + sh_ref[0]
        h_ref[...] = h.astype(h_ref.dtype)

    o_ref[...] = _mm(h_ref[...], w_ref[...])


def _proj_in(x2, mod3, gain, w, S):
    N, D = x2.shape
    NC = w.shape[1]
    tm = _pick(S, 1024, 8)
    tn = _pick(NC, 768)
    return pl.pallas_call(
        _proj_in_kernel,
        grid=(N // tm, NC // tn),
        in_specs=[
            pl.BlockSpec((tm, D), lambda i, j: (i, 0)),
            pl.BlockSpec((1, 1, D), lambda i, j: (i * tm // S, 0, 0)),
            pl.BlockSpec((1, 1, D), lambda i, j: (i * tm // S, 0, 1)),
            pl.BlockSpec((1, D), lambda i, j: (0, 0)),
            pl.BlockSpec((D, tn), lambda i, j: (0, j)),
        ],
        out_specs=pl.BlockSpec((tm, tn), lambda i, j: (i, j)),
        out_shape=jax.ShapeDtypeStruct((N, NC), f32),
        scratch_shapes=[pltpu.VMEM((tm, D), _MXU_DTYPE)],
        compiler_params=_cparams(("arbitrary", "arbitrary")),
        name="proj_in",
    )(x2, mod3, mod3, gain, w)


def _rot_tables(pos_ref, invf_ref, sign_ref):
    ang = pos_ref[...].astype(f32) * invf_ref[...]
    return jnp.cos(ang), jnp.sin(ang) * sign_ref[...]


def _rotate(x, cos_t, sin_t, half, period):
    lane = lax.broadcasted_iota(jnp.int32, x.shape, 1)
    partner = jnp.where((lane % period) < half,
                        pltpu.roll(x, LANES - half, 1),
                        pltpu.roll(x, half, 1))
    return x * cos_t + partner * sin_t


def _rope_consts(half, period, width):
    inv = ROPE_THETA ** (-jnp.arange(half, dtype=f32) / half)
    lane = np.arange(LANES)
    in_rot = ((lane % period) < 2 * half) & (lane < width)
    idx = (lane % period) % half
    invf = jnp.where(jnp.asarray(in_rot), inv[idx], 0.0).reshape(1, LANES).astype(f32)
    sign = np.where(in_rot, np.where((lane % period) < half, -1.0, 1.0), 0.0)
    return invf, jnp.asarray(sign, f32).reshape(1, LANES)


def _mla_up_kernel(cq_ref, ckv_ref, kr_ref, pos_ref, invf_ref, sign_ref, gq_ref, gkv_ref, wq_ref, wkv_ref,
                   q_ref, k_ref, v_ref, *, scale):
    cos_t, sin_t = _rot_tables(pos_ref, invf_ref, sign_ref)
    half = MLA_ROPE // 2
    qn = _rms(cq_ref[...], gq_ref[...]).astype(_MXU_DTYPE)
    q = _mm(qn, wq_ref[...]) * scale
    kvn = _rms(ckv_ref[...], gkv_ref[...]).astype(_MXU_DTYPE)
    kv = _mm(kvn, wkv_ref[...])
    k_rope = _rotate(kr_ref[...], cos_t, sin_t, half, MLA_ROPE).astype(k_ref.dtype)
    lane = lax.broadcasted_iota(jnp.int32, (q.shape[0], LANES), 1)
    H = MLA_HEADS
    for pair in range(H // 2):
        c0 = H * MLA_NOPE + pair * LANES
        r = _rotate(q[:, c0:c0 + LANES], cos_t, sin_t, half, MLA_ROPE)
        r_even = jnp.where(lane < MLA_ROPE, r, 0.0)
        r_odd = jnp.where(lane < MLA_ROPE, pltpu.roll(r, MLA_ROPE, 1), 0.0)
        for h, rr in ((2 * pair, r_even), (2 * pair + 1, r_odd)):
            q_ref[:, h * MLA_QK:h * MLA_QK + LANES] = q[:, h * MLA_NOPE:(h + 1) * MLA_NOPE].astype(q_ref.dtype)
            q_ref[:, h * MLA_QK + LANES:(h + 1) * MLA_QK] = rr.astype(q_ref.dtype)
    for h in range(H):
        k_ref[:, h * MLA_QK:h * MLA_QK + LANES] = kv[:, h * MLA_NOPE:(h + 1) * MLA_NOPE].astype(k_ref.dtype)
        k_ref[:, h * MLA_QK + LANES:(h + 1) * MLA_QK] = k_rope
    v_ref[...] = kv[:, H * MLA_NOPE:].astype(v_ref.dtype)


def _mla_up(p, pos, invf, sign, gq, gkv, wq, wkv, S):
    N = p.shape[0]
    tm = _pick(S, 512, 8)
    H = MLA_HEADS
    kern = functools.partial(_mla_up_kernel, scale=(MLA_NOPE + MLA_ROPE) ** -0.5)
    row = lambda i: (0, 0)
    return pl.pallas_call(
        kern,
        grid=(N // tm,),
        in_specs=[
            pl.BlockSpec((tm, MLA_Q_LORA), lambda i: (i, COL_CQ // MLA_Q_LORA)),
            pl.BlockSpec((tm, MLA_KV_LORA), lambda i: (i, COL_CKV // MLA_KV_LORA)),
            pl.BlockSpec((tm, LANES), lambda i: (i, COL_KR // LANES)),
            pl.BlockSpec((tm, 1), lambda i: (i, 0)),
            pl.BlockSpec((1, LANES), row),
            pl.BlockSpec((1, LANES), row),
            pl.BlockSpec((1, MLA_Q_LORA), row),
            pl.BlockSpec((1, MLA_KV_LORA), row),
            pl.BlockSpec(wq.shape, row),
            pl.BlockSpec(wkv.shape, row),
        ],
        out_specs=[
            pl.BlockSpec((tm, H * MLA_QK), lambda i: (i, 0)),
            pl.BlockSpec((tm, H * MLA_QK), lambda i: (i, 0)),
            pl.BlockSpec((tm, H * MLA_V), lambda i: (i, 0)),
        ],
        out_shape=[
            jax.ShapeDtypeStruct((N, H * MLA_QK), _MXU_DTYPE),
            jax.ShapeDtypeStruct((N, H * MLA_QK), _MXU_DTYPE),
            jax.ShapeDtypeStruct((N, H * MLA_V), _MXU_DTYPE),
        ],
        compiler_params=_cparams(("arbitrary",)),
        name="mla_up",
    )(p, p, p, pos, invf, sign, gq, gkv, wq, wkv)


def _softmax_init(m_ref, acc_ref):
    m_ref[...] = jnp.full(m_ref.shape, NEG, f32)
    acc_ref[...] = jnp.zeros(acc_ref.shape, f32)


def _softmax_step(s_ref, slot, tkc, v, m_ref, a_ref, p_ref, acc_ref, rc):
    rows = m_ref.shape[0]
    nc = tkc // LANES
    col = lambda rs, c: s_ref[slot, rs, c * LANES:(c + 1) * LANES]
    for r0 in range(0, rows, rc):
        rs = slice(r0, r0 + rc)
        m = m_ref[rs]
        m_new = jnp.maximum(m, jnp.max(functools.reduce(jnp.maximum, [col(rs, c) for c in range(nc)]),
                                       axis=1, keepdims=True))
        m_ref[rs] = m_new
        a_ref[rs] = jnp.exp(m - m_new)
    for r0 in range(0, rows, rc):
        rs = slice(r0, r0 + rc)
        m_new = m_ref[rs]
        psum = None
        for c in range(nc):
            pc = jnp.exp(col(rs, c) - m_new)
            p_ref[rs, c * LANES:(c + 1) * LANES] = pc.astype(p_ref.dtype)
            psum = pc if psum is None else psum + pc
        acc_ref[rs, LANES:] = a_ref[rs] * acc_ref[rs, LANES:] + jnp.sum(psum, axis=1, keepdims=True)
    acc_ref[:, :LANES] = a_ref[...] * acc_ref[:, :LANES] + _mm(p_ref[:, :tkc], v)


def _softmax_result(acc_ref):
    return acc_ref[:, :LANES] / acc_ref[:, LANES:]


def _tile_pipeline(n, produce, consume, finish_last=None):
    finish = (lambda slot: None) if finish_last is None else finish_last
    odd = n % 2

    @pl.when(n == 1)
    def _():
        produce(0, 0)
        finish(0)
        consume(0, 0)

    @pl.when(n >= 2)
    def _():
        produce(0, 0)

        def pair(jj, carry):
            j = 2 * jj
            produce(j + 1, 1)
            consume(j, 0)
            produce(j + 2, 0)
            consume(j + 1, 1)
            return carry

        lax.fori_loop(0, (n - 2 - odd) // 2, pair, 0)

        @pl.when(odd == 0)
        def _():
            produce(n - 1, 1)
            finish(1)
            consume(n - 2, 0)
            consume(n - 1, 1)

        @pl.when(odd == 1)
        def _():
            produce(n - 2, 1)
            consume(n - 3, 0)
            produce(n - 1, 0)
            finish(0)
            consume(n - 2, 1)
            consume(n - 1, 0)


def _flash_kernel(*refs, tq, tk, rc, scale, decay):
    if decay:
        q_ref, k_ref, v_ref, cq_ref, ck_ref, o_ref, s_ref, p_ref, m_ref, a_ref, acc_ref = refs
    else:
        q_ref, k_ref, v_ref, o_ref, s_ref, p_ref, m_ref, a_ref, acc_ref = refs
    i = pl.program_id(2)
    q = q_ref[...]
    if scale != 1.0:
        q = q.astype(f32) * scale
    q = q.astype(_MXU_DTYPE)
    if decay:
        cq = cq_ref[...]

    def produce(kt, slot):
        k0 = pl.multiple_of(kt * tk, tk)
        s = _mm_nt(q, k_ref[pl.ds(k0, tk), :].astype(_MXU_DTYPE))
        if decay:
            ck = ck_ref[0, 0, :, pl.ds(k0, tk)]
            for c in range(tk // LANES):
                cs = slice(c * LANES, (c + 1) * LANES)
                s_ref[slot, :, cs] = (s[:, cs] - ck[:, cs]) + cq
        else:
            s_ref[slot] = s

    def mask_diagonal(slot):
        col = lax.broadcasted_iota(jnp.int32, (1, tk), 1)
        for r0 in range(0, tq, rc):
            row = r0 + lax.broadcasted_iota(jnp.int32, (rc, 1), 0)
            s_ref[slot, r0:r0 + rc, :] = jnp.where(col <= row, s_ref[slot, r0:r0 + rc, :], NEG)

    def consume(kt, slot):
        k0 = pl.multiple_of(kt * tk, tk)
        _softmax_step(s_ref, slot, tk, v_ref[pl.ds(k0, tk), :].astype(_MXU_DTYPE),
                      m_ref, a_ref, p_ref, acc_ref, rc)

    _softmax_init(m_ref, acc_ref)
    _tile_pipeline(i + 1, produce, consume, mask_diagonal)
    o_ref[...] = _softmax_result(acc_ref).astype(o_ref.dtype)


def _flash(q, k, v, *, B, S, H, dk, dv, qcol, kcol, vcol, scale, cum_col=None, cum_row=None):
    N = B * S
    tq = _pick(S, 512, 8)
    tk = _pick(S, 512, LANES)
    rc = _pick(tq, 64, 8)
    nq = S // tq
    assert tq == tk and dv == LANES
    decay = cum_col is not None
    in_specs = [
        pl.BlockSpec((tq, dk), lambda b, h, i: (b * nq + i, qcol + h)),
        pl.BlockSpec((S, dk), lambda b, h, i: (b, kcol + h)),
        pl.BlockSpec((S, dv), lambda b, h, i: (b, vcol + h)),
    ]
    args = [q, k, v]
    if decay:
        in_specs += [
            pl.BlockSpec((tq, LANES), lambda b, h, i: (b * nq + i, h)),
            pl.BlockSpec((1, 1, 1, S), lambda b, h, i: (b, h, 0, 0)),
        ]
        args += [cum_col, cum_row]
    kern = functools.partial(_flash_kernel, tq=tq, tk=tk, rc=rc, scale=scale, decay=decay)
    return pl.pallas_call(
        kern,
        grid=(B, H, nq),
        in_specs=in_specs,
        out_specs=pl.BlockSpec((tq, dv), lambda b, h, i: (b * nq + i, h)),
        out_shape=jax.ShapeDtypeStruct((N, H * dv), f32),
        scratch_shapes=[
            pltpu.VMEM((2, tq, tk), f32),
            pltpu.VMEM((tq, tk), _MXU_DTYPE),
            pltpu.VMEM((tq, LANES), f32),
            pltpu.VMEM((tq, LANES), f32),
            pltpu.VMEM((tq, 2 * LANES), f32),
        ],
        compiler_params=_cparams(("arbitrary", "arbitrary", "arbitrary")),
        name="flash_fox" if decay else "flash_mla",
    )(*args)


def _fox_prep_kernel(ff_ref, b_ref, o_ref, rep_ref, *, chunk):
    S = ff_ref.shape[0]
    r = lax.broadcasted_iota(jnp.int32, (chunk, chunk), 0)
    c = lax.broadcasted_iota(jnp.int32, (chunk, chunk), 1)
    tri = (c <= r).astype(_MXU_DTYPE)

    def body(j, carry):
        r0 = pl.multiple_of(j * chunk, chunk)
        x = ff_ref[pl.ds(r0, chunk), :] + b_ref[...]
        lf = jnp.minimum(x, 0.0) - jnp.log(1.0 + jnp.exp(-jnp.abs(x)))
        hi, mid, lo = _split3(lf)
        cum = (_mm(tri, hi) + _mm(tri, mid)) + _mm(tri, lo) + carry
        o_ref[pl.ds(r0, chunk), :] = cum
        for h in range(FOX_HEADS):
            rep_ref[pl.ds(r0, chunk), h * LANES:(h + 1) * LANES] = jnp.broadcast_to(cum[:, h:h + 1], (chunk, LANES))
        return cum[chunk - 1:chunk, :]

    lax.fori_loop(0, S // chunk, body, jnp.zeros((1, LANES), f32))


def _fox_prep(p, b_pad, B, S):
    N = B * S
    kern = functools.partial(_fox_prep_kernel, chunk=_pick(S, 128, 8))
    return pl.pallas_call(
        kern,
        grid=(B,),
        in_specs=[
            pl.BlockSpec((S, LANES), lambda b: (b, COL_FF // LANES)),
            pl.BlockSpec((1, LANES), lambda b: (0, 0)),
        ],
        out_specs=[pl.BlockSpec((S, LANES), lambda b: (b, 0)),
                   pl.BlockSpec((S, FOX_HEADS * LANES), lambda b: (b, 0))],
        out_shape=[jax.ShapeDtypeStruct((N, LANES), f32),
                   jax.ShapeDtypeStruct((N, FOX_HEADS * LANES), f32)],
        compiler_params=_cparams(("arbitrary",)),
        name="fox_prep",
    )(p, b_pad)


def _nsa_prep_kernel(q_ref, kc_ref, vc_ref, ks_ref, vs_ref, kw_ref, vw_ref, pos_ref, invf_ref, sign_ref, blk_ref,
                     qo_ref, cmp_ref, kso_ref, vso_ref, kwo_ref, vwo_ref, kc_scr, *, scale):
    cos_t, sin_t = _rot_tables(pos_ref, invf_ref, sign_ref)
    half = PARTIAL_ROT // 2
    rot = lambda x: _rotate(x, cos_t, sin_t, half, LANES)
    for h in range(NSA_HEADS):
        sl = slice(h * HEAD_DIM, (h + 1) * HEAD_DIM)
        qo_ref[:, sl] = (rot(q_ref[:, sl]) * scale).astype(qo_ref.dtype)
    nrow = cmp_ref.shape[2]
    for g in range(NSA_KV_GROUPS):
        sl = slice(g * HEAD_DIM, (g + 1) * HEAD_DIM)
        kso_ref[:, 2 * g * HEAD_DIM:(2 * g + 1) * HEAD_DIM] = rot(ks_ref[:, sl]).astype(kso_ref.dtype)
        kso_ref[:, (2 * g + 1) * HEAD_DIM:(2 * g + 2) * HEAD_DIM] = blk_ref[...]
        kwo_ref[:, sl] = rot(kw_ref[:, sl]).astype(kwo_ref.dtype)
        kc_scr[0] = rot(kc_ref[:, sl])
        kc_scr[1] = vc_ref[:, sl]
        for t in range(NSA_CMP_STRIDE):
            cs = slice(t * HEAD_DIM, (t + 1) * HEAD_DIM)
            for kv in range(2):
                cmp_ref[kv, g, :, cs] = kc_scr[kv, pl.ds(t, nrow, stride=NSA_CMP_STRIDE), :].astype(cmp_ref.dtype)
    vso_ref[...] = vs_ref[...].astype(vso_ref.dtype)
    vwo_ref[...] = vw_ref[...].astype(vwo_ref.dtype)


def _nsa_prep(p, pos, invf, sign, expand, S):
    N = p.shape[0]
    tm = _pick(S, 512, 8)
    nt = S // tm
    G = NSA_KV_GROUPS
    cw = NSA_CMP_STRIDE * HEAD_DIM
    kv = lambda col: pl.BlockSpec((tm, NSA_KV), lambda i: (i, col // NSA_KV))
    row = lambda i: (0, 0)
    kern = functools.partial(_nsa_prep_kernel, scale=HEAD_DIM ** -0.5)
    return pl.pallas_call(
        kern,
        grid=(N // tm,),
        in_specs=[
            pl.BlockSpec((tm, W_NSA), lambda i: (i, COL_NQ // W_NSA)),
            kv(COL_NKC), kv(COL_NVC), kv(COL_NKS), kv(COL_NVS), kv(COL_NKW), kv(COL_NVW),
            pl.BlockSpec((tm, 1), lambda i: (i, 0)),
            pl.BlockSpec((1, LANES), row),
            pl.BlockSpec((1, LANES), row),
            pl.BlockSpec((tm, LANES), lambda i: (i % nt, 0)),
        ],
        out_specs=[
            pl.BlockSpec((tm, W_NSA), lambda i: (i, 0)),
            pl.BlockSpec((2, G, tm // NSA_CMP_STRIDE, cw), lambda i: (0, 0, i, 0)),
            pl.BlockSpec((tm, 2 * NSA_KV), lambda i: (i, 0)),
            pl.BlockSpec((tm, NSA_KV), lambda i: (i, 0)),
            pl.BlockSpec((tm, NSA_KV), lambda i: (i, 0)),
            pl.BlockSpec((tm, NSA_KV), lambda i: (i, 0)),
        ],
        out_shape=[
            jax.ShapeDtypeStruct((N, W_NSA), _MXU_DTYPE),
            jax.ShapeDtypeStruct((2, G, N // NSA_CMP_STRIDE, cw), _MXU_DTYPE),
            jax.ShapeDtypeStruct((N, 2 * NSA_KV), _MXU_DTYPE),
            jax.ShapeDtypeStruct((N, NSA_KV), _MXU_DTYPE),
            jax.ShapeDtypeStruct((N, NSA_KV), _MXU_DTYPE),
            jax.ShapeDtypeStruct((N, NSA_KV), _MXU_DTYPE),
        ],
        scratch_shapes=[pltpu.VMEM((2, tm, HEAD_DIM), f32)],
        compiler_params=_cparams(("arbitrary",)),
        name="nsa_prep",
    )(p, p, p, p, p, p, p, pos, invf, sign, expand)


def _compress_kernel(c_ref, pos_ref, w1_ref, w2_ref, o_ref):
    x = c_ref[0, 0, 0]
    half = x.shape[1]
    nrow = x.shape[0]
    a = _mm(x, w1_ref[0, :half, :])
    b = _mm(x, w1_ref[0, half:, :])
    posb = _mm(pos_ref[0].astype(_MXU_DTYPE), w1_ref[0])[0:1, :]
    hid = a + pltpu.roll(b, nrow - 1, 0) + posb
    c0 = math.sqrt(2.0 / math.pi)
    act = 0.5 * hid * (1.0 + jnp.tanh(c0 * (hid + 0.044715 * (hid * hid * hid))))
    o_ref[0, 0, 0] = _mm(act.astype(_MXU_DTYPE), w2_ref[0])


def _compress(cmp_in, pos_flat, w1, w2, B, S):
    G = NSA_KV_GROUPS
    nch = S // NSA_CMP_STRIDE
    width = NSA_CMP_STRIDE * HEAD_DIM
    x = cmp_in.reshape(2, G, B, nch, width)
    return pl.pallas_call(
        _compress_kernel,
        grid=(2, G, B),
        in_specs=[
            pl.BlockSpec((1, 1, 1, nch, width), lambda t, g, b: (t, g, b, 0, 0)),
            pl.BlockSpec((1, 8, 2 * width), lambda t, g, b: (t, 0, 0)),
            pl.BlockSpec((1, 2 * width, NSA_CMP_HIDDEN), lambda t, g, b: (t, 0, 0)),
            pl.BlockSpec((1, NSA_CMP_HIDDEN, HEAD_DIM), lambda t, g, b: (t, 0, 0)),
        ],
        out_specs=pl.BlockSpec((1, 1, 1, nch, HEAD_DIM), lambda t, g, b: (t, g, b, 0, 0)),
        out_shape=jax.ShapeDtypeStruct((2, G, B, nch, HEAD_DIM), f32),
        compiler_params=_cparams(("arbitrary", "arbitrary", "arbitrary")),
        name="nsa_compress",
    )(x, pos_flat, w1, w2)


def _nsa_attn_kernel(q_ref, kc_ref, vc_ref, ks_ref, vs_ref, kw_ref, vw_ref, g_ref, ovl_ref, o_ref,
                     s_ref, p_ref, m_ref, a_ref, acc_ref, out_scr, psum_ref,
                     *, tq, tks, tkw, rc, n_slc, top_k):
    i = pl.program_id(2)
    Hg = NSA_GQA
    D = HEAD_DIM
    q4 = q_ref[...]
    qs = jnp.concatenate([q4[:, h * D:(h + 1) * D] for h in range(Hg)], axis=0)
    t = i * tq + lax.broadcasted_iota(jnp.int32, (tq, 1), 0)
    gate = jax.nn.sigmoid(g_ref[...])
    head_rows = lambda h: slice(h * tq, (h + 1) * tq)

    kc = kc_ref[0, 0, 0].astype(_MXU_DTYPE)
    vc = vc_ref[0, 0, 0].astype(_MXU_DTYPE)
    ncmp = kc.shape[0]
    cend = lax.broadcasted_iota(jnp.int32, (1, ncmp), 1) * NSA_CMP_STRIDE + (NSA_CMP_LEN - 1)
    s_ref[0, :, :ncmp] = _mm_nt(qs, kc)
    cq = LANES
    for h in range(Hg):
        for q0 in range(0, tq, cq):
            rs = slice(h * tq + q0, h * tq + q0 + cq)
            tc = t[q0:q0 + cq]
            s = jnp.where(cend <= tc, s_ref[0, rs, :ncmp], NEG)
            e = jnp.exp(s - jnp.max(s, axis=1, keepdims=True))
            l = jnp.sum(e, axis=1, keepdims=True)
            p = e * jnp.where(tc >= NSA_CMP_LEN - 1, 1.0 / l, 0.0)
            p_ref[rs, :ncmp] = p.astype(p_ref.dtype)
            if h == 0:
                psum_ref[q0:q0 + cq] = p
            else:
                psum_ref[q0:q0 + cq] += p
    o_cmp = _mm(p_ref[:, :ncmp], vc)
    for h in range(Hg):
        out_scr[head_rows(h)] = gate[:, 3 * h:3 * h + 1] * o_cmp[head_rows(h)]

    hi, mid, lo = _split3(psum_ref[...])
    ovl = ovl_ref[...]
    imp = (_mm(hi, ovl) + _mm(mid, ovl)) + _mm(lo, ovl)
    nb = -(-n_slc // 8) * 8
    x = imp.T[:nb]
    jr = lax.broadcasted_iota(jnp.int32, (nb, 1), 0)
    cur_l = (i * tq + lax.broadcasted_iota(jnp.int32, (1, tq), 1)) // NSA_SLC_BLOCK
    forced = (jr == 0) | (jr == cur_l) | (jr == cur_l - 1)
    x = jnp.where(forced, BIG, jnp.where(jr > cur_l, NEG, x))
    xg = [x[g0:g0 + 8] for g0 in range(0, nb, 8)]
    jg = lax.broadcasted_iota(jnp.int32, (8, 1), 0)
    rg = [jnp.zeros((8, tq), f32) for _ in xg]
    for kk in range(n_slc):
        ck = jnp.broadcast_to(x[kk:kk + 1, :], (8, tq))
        for g, xv in enumerate(xg):
            if 8 * g > kk:
                before = ck >= xv
            elif 8 * g + 7 < kk:
                before = ck > xv
            else:
                before = (ck > xv) | ((ck == xv) & (jg + 8 * g > kk))
            rg[g] = rg[g] + before.astype(f32)
    rank = jnp.concatenate(rg, axis=0)
    unsel_t = jnp.where((rank < float(top_k)) & (jr <= cur_l), 0.0, NEG)
    if nb < LANES:
        unsel_t = jnp.concatenate([unsel_t, jnp.zeros((LANES - nb, tq), f32)], axis=0)
    unsel = unsel_t.T.astype(_MXU_DTYPE)
    q_aug = jnp.concatenate([qs, jnp.concatenate([unsel] * Hg, axis=0)], axis=1)

    def store_masked(slot, s, allowed, width):
        bias = jnp.where(allowed, 0.0, NEG)
        for h in range(Hg):
            s_ref[slot, head_rows(h), :width] = s[head_rows(h)] + bias

    def add_gated(c):
        o = _softmax_result(acc_ref)
        for h in range(Hg):
            out_scr[head_rows(h)] += gate[:, 3 * h + c:3 * h + c + 1] * o[head_rows(h)]

    n_ts = ((i + 1) * tq + tks - 1) // tks

    def slc_produce(kt, slot):
        k0 = pl.multiple_of(kt * tks, tks)
        s_ref[slot] = _mm_nt(q_aug, ks_ref[pl.ds(k0, tks), :])

    def slc_mask_last(slot):
        col = (n_ts - 1) * tks + lax.broadcasted_iota(jnp.int32, (1, tks), 1)
        for r0 in range(0, Hg * tq, rc):
            q0 = r0 % tq
            s_ref[slot, r0:r0 + rc, :] = jnp.where(col <= t[q0:q0 + rc], s_ref[slot, r0:r0 + rc, :], NEG)

    def slc_consume(kt, slot):
        k0 = pl.multiple_of(kt * tks, tks)
        _softmax_step(s_ref, slot, tks, vs_ref[pl.ds(k0, tks), :], m_ref, a_ref, p_ref, acc_ref, rc)

    _softmax_init(m_ref, acc_ref)
    _tile_pipeline(n_ts, slc_produce, slc_consume, slc_mask_last)
    add_gated(1)

    hi_t = (i * tq + tq - 1) // tkw
    lo_t = jnp.maximum(i * tq - (NSA_WINDOW - 1), 0) // tkw

    def win_produce(j, slot):
        k0 = pl.multiple_of((hi_t - j) * tkw, tkw)
        col = k0 + lax.broadcasted_iota(jnp.int32, (1, tkw), 1)
        store_masked(slot, _mm_nt(qs, kw_ref[pl.ds(k0, tkw), :]), (col <= t) & (t - col < NSA_WINDOW), tkw)

    def win_consume(j, slot):
        k0 = pl.multiple_of((hi_t - j) * tkw, tkw)
        _softmax_step(s_ref, slot, tkw, vw_ref[pl.ds(k0, tkw), :], m_ref, a_ref, p_ref, acc_ref,
                      rc * (tks // tkw))

    _softmax_init(m_ref, acc_ref)
    _tile_pipeline(hi_t - lo_t + 1, win_produce, win_consume)
    add_gated(2)
    for h in range(Hg):
        o_ref[:, h * D:(h + 1) * D] = out_scr[head_rows(h)].astype(o_ref.dtype)


def _nsa_attn(qn, cmp_kv, ks, vs, kw, vw, p, ovl, B, S):
    N = B * S
    G = NSA_KV_GROUPS
    D = HEAD_DIM
    tq = _pick(S, 256, LANES)
    tks = _pick(S, 512, LANES)
    tkw = _pick(S, 256, LANES)
    rc = _pick(tq, 64, 8)
    nq = S // tq
    nch = cmp_kv.shape[3]
    n_slc = S // NSA_SLC_BLOCK
    assert n_slc <= LANES and nch <= tks and tkw <= tks
    kern = functools.partial(_nsa_attn_kernel, tq=tq, tks=tks, tkw=tkw, rc=rc, n_slc=n_slc,
                             top_k=min(NSA_TOPK, n_slc))
    R = NSA_GQA * tq
    tk = tks
    seq = lambda: pl.BlockSpec((S, D), lambda b, g, i: (b, g))
    return pl.pallas_call(
        kern,
        grid=(B, G, nq),
        in_specs=[
            pl.BlockSpec((tq, NSA_GQA * D), lambda b, g, i: (b * nq + i, g)),
            pl.BlockSpec((1, 1, 1, nch, D), lambda b, g, i: (0, g, b, 0, 0)),
            pl.BlockSpec((1, 1, 1, nch, D), lambda b, g, i: (1, g, b, 0, 0)),
            pl.BlockSpec((S, 2 * D), lambda b, g, i: (b, g)),
            seq(), seq(), seq(),
            pl.BlockSpec((tq, LANES), lambda b, g, i: (b * nq + i, COL_NG // LANES + g)),
            pl.BlockSpec(ovl.shape, lambda b, g, i: (0, 0)),
        ],
        out_specs=pl.BlockSpec((tq, NSA_GQA * D), lambda b, g, i: (b * nq + i, g)),
        out_shape=jax.ShapeDtypeStruct((N, W_NSA), f32),
        scratch_shapes=[
            pltpu.VMEM((2, R, tk), f32),
            pltpu.VMEM((R, tk), _MXU_DTYPE),
            pltpu.VMEM((R, LANES), f32),
            pltpu.VMEM((R, LANES), f32),
            pltpu.VMEM((R, 2 * D), f32),
            pltpu.VMEM((R, D), f32),
            pltpu.VMEM((tq, nch), f32),
        ],
        compiler_params=_cparams(("arbitrary", "arbitrary", "arbitrary")),
        name="nsa_attn",
    )(qn, cmp_kv, cmp_kv, ks, vs, kw, vw, p, ovl)


def _nsa_consts(S):
    nch = S // NSA_CMP_STRIDE
    n_slc = S // NSA_SLC_BLOCK
    c_start = np.arange(nch) * NSA_CMP_STRIDE
    s_start = np.arange(n_slc) * NSA_SLC_BLOCK
    ov = (np.minimum(c_start[:, None] + NSA_CMP_LEN, s_start[None, :] + NSA_SLC_BLOCK)
          - np.maximum(c_start[:, None], s_start[None, :]))
    ovl = np.zeros((nch, LANES), np.float32)
    ovl[:, :n_slc] = np.clip(ov, 0, None) / NSA_CMP_LEN
    expand = np.zeros((S, LANES), np.float32)
    expand[np.arange(S), np.arange(S) // NSA_SLC_BLOCK] = 1.0
    return jnp.asarray(ovl, _MXU_DTYPE), jnp.asarray(expand, _MXU_DTYPE)


def _out_proj_kernel(om_ref, of_ref, on_ref, gm_ref, gf_ref, gn_ref, w_ref, x_ref, gt_ref, o_ref, h_ref):
    @pl.when(pl.program_id(1) == 0)
    def _():
        h_ref[:, :W_MLA] = _rms(om_ref[...], gm_ref[...]).astype(h_ref.dtype)
        h_ref[:, W_MLA:W_MLA + W_FOX] = _rms(of_ref[...], gf_ref[...]).astype(h_ref.dtype)
        h_ref[:, W_MLA + W_FOX:] = _rms(on_ref[...], gn_ref[...]).astype(h_ref.dtype)

    o_ref[...] = x_ref[...] + gt_ref[0] * _mm(h_ref[...], w_ref[...])


def _out_proj(o_mla, o_fox, o_nsa, gm, gf, gn, w, x2, mod3, S):
    N, D = x2.shape
    DM = w.shape[0]
    tm = _pick(S, 512, 8)
    tn = _pick(D, 2048)
    nj = D // tn
    row = lambda i, j: (0, 0)
    return pl.pallas_call(
        _out_proj_kernel,
        grid=(N // tm, nj),
        in_specs=[
            pl.BlockSpec((tm, W_MLA), lambda i, j: (i, 0)),
            pl.BlockSpec((tm, W_FOX), lambda i, j: (i, 0)),
            pl.BlockSpec((tm, W_NSA), lambda i, j: (i, 0)),
            pl.BlockSpec((1, W_MLA), row),
            pl.BlockSpec((1, W_FOX), row),
            pl.BlockSpec((1, W_NSA), row),
            pl.BlockSpec((DM, tn), lambda i, j: (0, j)),
            pl.BlockSpec((tm, tn), lambda i, j: (i, j)),
            pl.BlockSpec((1, 1, tn), lambda i, j: (i * tm // S, 0, 2 * nj + j)),
        ],
        out_specs=pl.BlockSpec((tm, tn), lambda i, j: (i, j)),
        out_shape=jax.ShapeDtypeStruct((N, D), f32),
        scratch_shapes=[pltpu.VMEM((tm, DM), _MXU_DTYPE)],
        compiler_params=_cparams(("arbitrary", "arbitrary")),
        name="out_proj",
    )(o_mla, o_fox, o_nsa, gm, gf, gn, w, x2, mod3)


def _mlp_kernel(x_ref, sh_ref, sc_ref, gt_ref, g_ref, w1_ref, w2_ref, fg_ref, o_ref, h_ref, acc_ref, *, final):
    f = pl.program_id(1)

    @pl.when(f == 0)
    def _():
        h = _rms(x_ref[...], g_ref[...]) * (1.0 + sc_ref[0]) + sh_ref[0]
        h_ref[...] = h.astype(h_ref.dtype)
        acc_ref[...] = jnp.zeros_like(acc_ref)

    u = jnp.maximum(_mm(h_ref[...], w1_ref[...]), 0.0)
    acc_ref[...] += _mm((u * u).astype(_MXU_DTYPE), w2_ref[...])

    @pl.when(f == pl.num_programs(1) - 1)
    def _():
        y = x_ref[...] + gt_ref[0] * acc_ref[...]
        if final:
            y = _rms(y, fg_ref[...])
        o_ref[...] = y


def _mlp(x2, mod3, gain, w1, w2, final_gain, S, final):
    N, D = x2.shape
    F = w1.shape[1]
    tm = _pick(S, 512, 8)
    tf = _pick(F, 1024)
    row = lambda i, f: (0, 0)
    mod_blk = lambda c: pl.BlockSpec((1, 1, D), lambda i, f: (i * tm // S, 0, c))
    return pl.pallas_call(
        functools.partial(_mlp_kernel, final=final),
        grid=(N // tm, F // tf),
        in_specs=[
            pl.BlockSpec((tm, D), lambda i, f: (i, 0)),
            mod_blk(3), mod_blk(4), mod_blk(5),
            pl.BlockSpec((1, D), row),
            pl.BlockSpec((D, tf), lambda i, f: (0, f)),
            pl.BlockSpec((tf, D), lambda i, f: (f, 0)),
            pl.BlockSpec((1, D), row),
        ],
        out_specs=pl.BlockSpec((tm, D), lambda i, f: (i, 0)),
        out_shape=jax.ShapeDtypeStruct((N, D), f32),
        scratch_shapes=[pltpu.VMEM((tm, D), _MXU_DTYPE), pltpu.VMEM((tm, D), f32)],
        compiler_params=_cparams(("arbitrary", "arbitrary")),
        name="mlp",
    )(x2, mod3, mod3, mod3, gain, w1, w2, final_gain)


def _w_in_segments():
    widths = (MLA_Q_LORA, MLA_KV_LORA, MLA_ROPE, W_FOX, W_FOX, W_FOX, FOX_HEADS,
              W_NSA, NSA_KV, NSA_KV, NSA_KV, NSA_KV, NSA_KV, NSA_KV, 3 * NSA_HEADS)
    src = np.cumsum((0,) + widths)
    (cq, ckv, kr, fq, fk, fv, ff, nq, nkc, nvc, nks, nvs, nkw, nvw, ng) = [int(s) for s in src[:-1]]
    gw = 3 * NSA_GQA
    return [(COL_NQ, nq, W_NSA), (COL_CQ, cq, MLA_Q_LORA), (COL_FQ, fq, W_FOX), (COL_FK, fk, W_FOX),
            (COL_FV, fv, W_FOX), (COL_CKV, ckv, MLA_KV_LORA), (COL_NKC, nkc, NSA_KV), (COL_NVC, nvc, NSA_KV),
            (COL_NKS, nks, NSA_KV), (COL_NVS, nvs, NSA_KV), (COL_NKW, nkw, NSA_KV), (COL_NVW, nvw, NSA_KV),
            (COL_KR, kr, MLA_ROPE), (COL_FF, ff, FOX_HEADS), (COL_NG, ng, gw), (COL_NG + LANES, ng + gw, gw)]


def _w_in_regroup_kernel(w_ref, o_ref):
    o_ref[...] = jnp.zeros(o_ref.shape, o_ref.dtype)
    for dst, src, width in _w_in_segments():
        o_ref[:, dst:dst + width] = w_ref[0, :, src:src + width].astype(o_ref.dtype)


def _prep_w_in(w_in, l):
    _, D, C = w_in.shape
    tr = _pick(D, 256, 8)
    return pl.pallas_call(
        _w_in_regroup_kernel,
        grid=(D // tr,),
        in_specs=[pl.BlockSpec((1, tr, C), lambda i: (l, i, 0))],
        out_specs=pl.BlockSpec((tr, N_COLS), lambda i: (i, 0)),
        out_shape=jax.ShapeDtypeStruct((D, N_COLS), _MXU_DTYPE),
        compiler_params=_cparams(("arbitrary",)),
        name="w_in_regroup",
    )(w_in)


def _prep_mla_w(w_uq, w_ukv):
    H = MLA_HEADS
    qd = MLA_NOPE + MLA_ROPE
    q_nope = np.concatenate([np.arange(h * qd, h * qd + MLA_NOPE) for h in range(H)])
    q_rope = np.concatenate([np.arange(h * qd + MLA_NOPE, (h + 1) * qd) for h in range(H)])
    kd = MLA_NOPE + MLA_V
    k_nope = np.concatenate([np.arange(h * kd, h * kd + MLA_NOPE) for h in range(H)])
    v_cols = np.concatenate([np.arange(h * kd + MLA_NOPE, (h + 1) * kd) for h in range(H)])
    wq = jnp.take(w_uq, jnp.asarray(np.concatenate([q_nope, q_rope])), axis=1).astype(_MXU_DTYPE)
    wkv = jnp.take(w_ukv, jnp.asarray(np.concatenate([k_nope, v_cols])), axis=1).astype(_MXU_DTYPE)
    return wq, wkv


def _pad_lanes(v):
    return jnp.zeros((1, LANES), f32).at[0, :v.shape[0]].set(v.astype(f32))


def kernel(x, c, positions, ada_w, ada_b, norm_mix, norm_mlp, w_in, fox_b_f, mla_q_norm, mla_w_uq, mla_kv_norm,
           mla_w_ukv, nsa_pos_k, nsa_pos_v, nsa_cmp_k_w1, nsa_cmp_k_w2, nsa_cmp_v_w1, nsa_cmp_v_w2, out_norm,
           w_out, mlp_w1, mlp_w2, final_norm):
    B, S, D = x.shape
    L = ada_w.shape[0]
    N = B * S
    assert S % LANES == 0 and D % LANES == 0 and B <= 8

    x2 = x.reshape(N, D)
    pos = positions.reshape(N, 1).astype(jnp.int32)
    c_pad = jnp.zeros((8, D), f32).at[:B].set(c)
    mod = _ada_mod(c_pad, ada_w, ada_b)

    invf_mla, sign_mla = _rope_consts(MLA_ROPE // 2, MLA_ROPE, LANES)
    invf_nsa, sign_nsa = _rope_consts(PARTIAL_ROT // 2, LANES, LANES)
    ovl, expand = _nsa_consts(S)
    row = lambda v: v.reshape(1, -1)

    for l in range(L):
        mod3 = mod[l, :B].reshape(B, 1, 6 * D)
        p = _proj_in(x2, mod3, row(norm_mix[l]), _prep_w_in(w_in, l), S)

        wq, wkv = _prep_mla_w(mla_w_uq[l], mla_w_ukv[l])
        q_m, k_m, v_m = _mla_up(p, pos, invf_mla, sign_mla, row(mla_q_norm[l]), row(mla_kv_norm[l]), wq, wkv, S)
        o_mla = _flash(q_m, k_m, v_m, B=B, S=S, H=MLA_HEADS, dk=MLA_QK, dv=MLA_V, qcol=0, kcol=0, vcol=0,
                       scale=1.0)

        cum, cum_rep = _fox_prep(p, _pad_lanes(fox_b_f[l]), B, S)
        cum_row = cum[:, :FOX_HEADS].reshape(B, S, FOX_HEADS).transpose(0, 2, 1)[:, :, None, :]
        o_fox = _flash(p, p, p, B=B, S=S, H=FOX_HEADS, dk=HEAD_DIM, dv=HEAD_DIM,
                       qcol=COL_FQ // HEAD_DIM, kcol=COL_FK // HEAD_DIM, vcol=COL_FV // HEAD_DIM,
                       scale=HEAD_DIM ** -0.5, cum_col=cum_rep, cum_row=cum_row)

        qn, cmp_in, ks, vs, kw, vw = _nsa_prep(p, pos, invf_nsa, sign_nsa, expand, S)
        flat = lambda a: jnp.broadcast_to(a.reshape(1, -1), (8, a.size))
        cmp_kv = _compress(
            cmp_in,
            jnp.stack([flat(nsa_pos_k[l]), flat(nsa_pos_v[l])]).astype(f32),
            jnp.stack([nsa_cmp_k_w1[l], nsa_cmp_v_w1[l]]).astype(_MXU_DTYPE),
            jnp.stack([nsa_cmp_k_w2[l], nsa_cmp_v_w2[l]]).astype(_MXU_DTYPE), B, S)
        o_nsa = _nsa_attn(qn, cmp_kv, ks, vs, kw, vw, p, ovl, B, S)

        gn = out_norm[l]
        x2 = _out_proj(o_mla, o_fox, o_nsa, row(gn[:W_MLA]), row(gn[W_MLA:W_MLA + W_FOX]),
                       row(gn[W_MLA + W_FOX:]), w_out[l].astype(_MXU_DTYPE), x2, mod3, S)
        x2 = _mlp(x2, mod3, row(norm_mlp[l]), mlp_w1[l].astype(_MXU_DTYPE), mlp_w2[l].astype(_MXU_DTYPE),
                  row(final_norm), S, final=(l == L - 1))
    return x2.reshape(B, S, D)
```

```python
import functools
import math

import numpy as np
import jax
import jax.numpy as jnp
from jax import lax
from jax.experimental import pallas as pl
from jax.experimental.pallas import tpu as pltpu

f32 = jnp.float32
_MXU_DTYPE = jnp.bfloat16

HEAD_DIM = 128
MLA_HEADS = 4
MLA_Q_LORA = 512
MLA_KV_LORA = 256
MLA_NOPE = 128
MLA_ROPE = 64
MLA_V = 128
FOX_HEADS = 4
NSA_HEADS = 8
NSA_KV_GROUPS = 2
NSA_GQA = NSA_HEADS // NSA_KV_GROUPS
NSA_CMP_LEN = 32
NSA_CMP_STRIDE = 16
NSA_CMP_HIDDEN = 512
NSA_SLC_BLOCK = 64
NSA_TOPK = 16
NSA_WINDOW = 512
W_MLA = MLA_HEADS * MLA_V
W_FOX = FOX_HEADS * HEAD_DIM
W_NSA = NSA_HEADS * HEAD_DIM
NSA_KV = NSA_KV_GROUPS * HEAD_DIM
ROPE_THETA = 500000.0
PARTIAL_ROT = HEAD_DIM // 4
EPS = 1e-6
NEG = -1e30
BIG = 1e30

LANES = 128
VMEM_LIMIT = 56 * 1024 * 1024

MLA_QK = 2 * LANES
COL_NQ = 0
COL_CQ = 1024
COL_FQ = 1536
COL_FK = 2048
COL_FV = 2560
COL_CKV = 3072
COL_NKC = 3328
COL_NVC = 3584
COL_NKS = 3840
COL_NVS = 4096
COL_NKW = 4352
COL_NVW = 4608
COL_KR = 4864
COL_FF = 4992
COL_NG = 5120
N_COLS = 5376
N_COLS16 = COL_NVW


def _pick(n, cap, mult=LANES):
    if n <= cap:
        return n
    t = (cap // mult) * mult
    while t >= mult:
        if n % t == 0:
            return t
        t -= mult
    raise ValueError(f"no tile for {n} under {cap}")


def _mm(a, b):
    return jnp.dot(a, b, preferred_element_type=f32)


def _mm_nt(a, b):
    return lax.dot_general(a, b, (((1,), (1,)), ((), ())), preferred_element_type=f32)


def _split3(a):
    hi = a.astype(_MXU_DTYPE)
    r1 = a - hi.astype(f32)
    mid = r1.astype(_MXU_DTYPE)
    lo = (r1 - mid.astype(f32)).astype(_MXU_DTYPE)
    return hi, mid, lo


def _rms(x, gain):
    return x * lax.rsqrt(jnp.mean(x * x, axis=-1, keepdims=True) + EPS) * gain


def _cparams(sem):
    return pltpu.CompilerParams(dimension_semantics=sem, vmem_limit_bytes=VMEM_LIMIT)


def _ada_kernel(c_ref, w_ref, b_ref, o_ref):
    c = c_ref[...]
    ca = c * jax.nn.sigmoid(c)
    w = w_ref[0]
    c_hi, c_mid, _ = _split3(ca)
    w_hi, w_mid, _ = _split3(w)
    acc = _mm(c_hi, w_hi) + (_mm(c_mid, w_hi) + _mm(c_hi, w_mid))
    o_ref[0] = acc + b_ref[0]


def _ada_mod(c_pad, ada_w, ada_b):
    L, D, D6 = ada_w.shape
    tn = _pick(D6, 1024)
    return pl.pallas_call(
        _ada_kernel,
        grid=(L, D6 // tn),
        in_specs=[
            pl.BlockSpec((8, D), lambda l, j: (0, 0)),
            pl.BlockSpec((1, D, tn), lambda l, j: (l, 0, j)),
            pl.BlockSpec((1, 1, tn), lambda l, j: (l, 0, j)),
        ],
        out_specs=pl.BlockSpec((1, 8, tn), lambda l, j: (l, 0, j)),
        out_shape=jax.ShapeDtypeStruct((L, 8, D6), f32),
        compiler_params=_cparams(("arbitrary", "arbitrary")),
        name="ada_mod",
    )(c_pad, ada_w, ada_b.reshape(L, 1, D6))


def _proj_in_kernel(x_ref, sh_ref, sc_ref, g_ref, w_ref, o16_ref, o32_ref, h_ref, *, n16):
    j = pl.program_id(1)

    @pl.when(j == 0)
    def _():
        h = _rms(x_ref[...], g_ref[...]) * (1.0 + sc_ref[0]) + sh_ref[0]
        h_ref[...] = h.astype(h_ref.dtype)

    r = _mm(h_ref[...], w_ref[...])

    @pl.when(j < n16)
    def _():
        o16_ref[...] = r.astype(o16_ref.dtype)

    @pl.when(j >= n16)
    def _():
        o32_ref[...] = r


def _proj_in(x2, mod3, gain, w, S):
    N, D = x2.shape
    NC = w.shape[1]
    tm = _pick(S, 1024, 8)
    tn = N_COLS - N_COLS16
    n16 = N_COLS16 // tn
    assert NC == N_COLS and N_COLS16 % tn == 0
    return pl.pallas_call(
        functools.partial(_proj_in_kernel, n16=n16),
        grid=(N // tm, NC // tn),
        in_specs=[
            pl.BlockSpec((tm, D), lambda i, j: (i, 0)),
            pl.BlockSpec((1, 1, D), lambda i, j: (i * tm // S, 0, 0)),
            pl.BlockSpec((1, 1, D), lambda i, j: (i * tm // S, 0, 1)),
            pl.BlockSpec((1, D), lambda i, j: (0, 0)),
            pl.BlockSpec((D, tn), lambda i, j: (0, j)),
        ],
        out_specs=[pl.BlockSpec((tm, tn), lambda i, j: (i, jnp.minimum(j, n16 - 1))),
                   pl.BlockSpec((tm, tn), lambda i, j: (i, 0))],
        out_shape=[jax.ShapeDtypeStruct((N, N_COLS16), _MXU_DTYPE),
                   jax.ShapeDtypeStruct((N, tn), f32)],
        scratch_shapes=[pltpu.VMEM((tm, D), _MXU_DTYPE)],
        compiler_params=_cparams(("arbitrary", "arbitrary")),
        name="proj_in",
    )(x2, mod3, mod3, gain, w)


def _rot_tables(pos_ref, invf_ref, sign_ref):
    ang = pos_ref[...].astype(f32) * invf_ref[...]
    return jnp.cos(ang), jnp.sin(ang) * sign_ref[...]


def _rotate(x, cos_t, sin_t, half, period):
    lane = lax.broadcasted_iota(jnp.int32, x.shape, 1)
    partner = jnp.where((lane % period) < half,
                        pltpu.roll(x, LANES - half, 1),
                        pltpu.roll(x, half, 1))
    return x * cos_t + partner * sin_t


def _rope_consts(half, period, width):
    inv = ROPE_THETA ** (-jnp.arange(half, dtype=f32) / half)
    lane = np.arange(LANES)
    in_rot = ((lane % period) < 2 * half) & (lane < width)
    idx = (lane % period) % half
    invf = jnp.where(jnp.asarray(in_rot), inv[idx], 0.0).reshape(1, LANES).astype(f32)
    sign = np.where(in_rot, np.where((lane % period) < half, -1.0, 1.0), 0.0)
    return invf, jnp.asarray(sign, f32).reshape(1, LANES)


def _mla_up_kernel(cq_ref, ckv_ref, kr_ref, pos_ref, invf_ref, sign_ref, gq_ref, gkv_ref, wq_ref, wkv_ref,
                   q_ref, k_ref, v_ref, *, scale):
    cos_t, sin_t = _rot_tables(pos_ref, invf_ref, sign_ref)
    half = MLA_ROPE // 2
    qn = _rms(cq_ref[...].astype(f32), gq_ref[...]).astype(_MXU_DTYPE)
    q = _mm(qn, wq_ref[...]) * scale
    kvn = _rms(ckv_ref[...].astype(f32), gkv_ref[...]).astype(_MXU_DTYPE)
    kv = _mm(kvn, wkv_ref[...])
    k_rope = _rotate(kr_ref[...], cos_t, sin_t, half, MLA_ROPE).astype(k_ref.dtype)
    lane = lax.broadcasted_iota(jnp.int32, (q.shape[0], LANES), 1)
    H = MLA_HEADS
    for pair in range(H // 2):
        c0 = H * MLA_NOPE + pair * LANES
        r = _rotate(q[:, c0:c0 + LANES], cos_t, sin_t, half, MLA_ROPE)
        r_even = jnp.where(lane < MLA_ROPE, r, 0.0)
        r_odd = jnp.where(lane < MLA_ROPE, pltpu.roll(r, MLA_ROPE, 1), 0.0)
        for h, rr in ((2 * pair, r_even), (2 * pair + 1, r_odd)):
            q_ref[:, h * MLA_QK:h * MLA_QK + LANES] = q[:, h * MLA_NOPE:(h + 1) * MLA_NOPE].astype(q_ref.dtype)
            q_ref[:, h * MLA_QK + LANES:(h + 1) * MLA_QK] = rr.astype(q_ref.dtype)
    for h in range(H):
        k_ref[:, h * MLA_QK:h * MLA_QK + LANES] = kv[:, h * MLA_NOPE:(h + 1) * MLA_NOPE].astype(k_ref.dtype)
        k_ref[:, h * MLA_QK + LANES:(h + 1) * MLA_QK] = k_rope
    v_ref[...] = kv[:, H * MLA_NOPE:].astype(v_ref.dtype)


def _mla_up(p, p32, pos, invf, sign, gq, gkv, wq, wkv, S):
    N = p.shape[0]
    tm = _pick(S, 512, 8)
    H = MLA_HEADS
    kern = functools.partial(_mla_up_kernel, scale=(MLA_NOPE + MLA_ROPE) ** -0.5)
    row = lambda i: (0, 0)
    return pl.pallas_call(
        kern,
        grid=(N // tm,),
        in_specs=[
            pl.BlockSpec((tm, MLA_Q_LORA), lambda i: (i, COL_CQ // MLA_Q_LORA)),
            pl.BlockSpec((tm, MLA_KV_LORA), lambda i: (i, COL_CKV // MLA_KV_LORA)),
            pl.BlockSpec((tm, LANES), lambda i: (i, (COL_KR - N_COLS16) // LANES)),
            pl.BlockSpec((tm, 1), lambda i: (i, 0)),
            pl.BlockSpec((1, LANES), row),
            pl.BlockSpec((1, LANES), row),
            pl.BlockSpec((1, MLA_Q_LORA), row),
            pl.BlockSpec((1, MLA_KV_LORA), row),
            pl.BlockSpec(wq.shape, row),
            pl.BlockSpec(wkv.shape, row),
        ],
        out_specs=[
            pl.BlockSpec((tm, H * MLA_QK), lambda i: (i, 0)),
            pl.BlockSpec((tm, H * MLA_QK), lambda i: (i, 0)),
            pl.BlockSpec((tm, H * MLA_V), lambda i: (i, 0)),
        ],
        out_shape=[
            jax.ShapeDtypeStruct((N, H * MLA_QK), _MXU_DTYPE),
            jax.ShapeDtypeStruct((N, H * MLA_QK), _MXU_DTYPE),
            jax.ShapeDtypeStruct((N, H * MLA_V), _MXU_DTYPE),
        ],
        compiler_params=_cparams(("arbitrary",)),
        name="mla_up",
    )(p, p, p32, pos, invf, sign, gq, gkv, wq, wkv)


def _softmax_init(m_ref, acc_ref):
    m_ref[...] = jnp.full(m_ref.shape, NEG, f32)
    acc_ref[...] = jnp.zeros(acc_ref.shape, f32)


def _softmax_step(s_ref, slot, tkc, v, m_ref, a_ref, p_ref, acc_ref, rc):
    rows = m_ref.shape[0]
    nc = tkc // LANES
    col = lambda rs, c: s_ref[slot, rs, c * LANES:(c + 1) * LANES]
    for r0 in range(0, rows, rc):
        rs = slice(r0, r0 + rc)
        m = m_ref[rs]
        m_new = jnp.maximum(m, jnp.max(functools.reduce(jnp.maximum, [col(rs, c) for c in range(nc)]),
                                       axis=1, keepdims=True))
        m_ref[rs] = m_new
        a_ref[rs] = jnp.exp(m - m_new)
    for r0 in range(0, rows, rc):
        rs = slice(r0, r0 + rc)
        m_new = m_ref[rs]
        psum = None
        for c in range(nc):
            pc = jnp.exp(col(rs, c) - m_new)
            p_ref[rs, c * LANES:(c + 1) * LANES] = pc.astype(p_ref.dtype)
            psum = pc if psum is None else psum + pc
        acc_ref[rs, LANES:] = a_ref[rs] * acc_ref[rs, LANES:] + jnp.sum(psum, axis=1, keepdims=True)
    acc_ref[:, :LANES] = a_ref[...] * acc_ref[:, :LANES] + _mm(p_ref[:, :tkc], v)


def _softmax_result(acc_ref):
    return acc_ref[:, :LANES] / acc_ref[:, LANES:]


def _tile_pipeline(n, produce, consume, finish_last=None):
    finish = (lambda slot: None) if finish_last is None else finish_last
    odd = n % 2

    @pl.when(n == 1)
    def _():
        produce(0, 0)
        finish(0)
        consume(0, 0)

    @pl.when(n >= 2)
    def _():
        produce(0, 0)

        def pair(jj, carry):
            j = 2 * jj
            produce(j + 1, 1)
            consume(j, 0)
            produce(j + 2, 0)
            consume(j + 1, 1)
            return carry

        lax.fori_loop(0, (n - 2 - odd) // 2, pair, 0)

        @pl.when(odd == 0)
        def _():
            produce(n - 1, 1)
            finish(1)
            consume(n - 2, 0)
            consume(n - 1, 1)

        @pl.when(odd == 1)
        def _():
            produce(n - 2, 1)
            consume(n - 3, 0)
            produce(n - 1, 0)
            finish(0)
            consume(n - 2, 1)
            consume(n - 1, 0)


def _flash_kernel(*refs, tq, tk, rc, scale, decay):
    if decay:
        q_ref, k_ref, v_ref, cq_ref, ck_ref, o_ref, s_ref, p_ref, m_ref, a_ref, acc_ref = refs
    else:
        q_ref, k_ref, v_ref, o_ref, s_ref, p_ref, m_ref, a_ref, acc_ref = refs
    i = pl.program_id(2)
    q = q_ref[...]
    if scale != 1.0:
        q = q.astype(f32) * scale
    q = q.astype(_MXU_DTYPE)
    if decay:
        cq = cq_ref[...]

    def produce(kt, slot):
        k0 = pl.multiple_of(kt * tk, tk)
        s = _mm_nt(q, k_ref[pl.ds(k0, tk), :].astype(_MXU_DTYPE))
        if decay:
            ck = ck_ref[0, 0, :, pl.ds(k0, tk)]
            for c in range(tk // LANES):
                cs = slice(c * LANES, (c + 1) * LANES)
                s_ref[slot, :, cs] = (s[:, cs] - ck[:, cs]) + cq
        else:
            s_ref[slot] = s

    def mask_diagonal(slot):
        col = lax.broadcasted_iota(jnp.int32, (1, tk), 1)
        for r0 in range(0, tq, rc):
            row = r0 + lax.broadcasted_iota(jnp.int32, (rc, 1), 0)
            s_ref[slot, r0:r0 + rc, :] = jnp.where(col <= row, s_ref[slot, r0:r0 + rc, :], NEG)

    def consume(kt, slot):
        k0 = pl.multiple_of(kt * tk, tk)
        _softmax_step(s_ref, slot, tk, v_ref[pl.ds(k0, tk), :].astype(_MXU_DTYPE),
                      m_ref, a_ref, p_ref, acc_ref, rc)

    _softmax_init(m_ref, acc_ref)
    _tile_pipeline(i + 1, produce, consume, mask_diagonal)
    o_ref[...] = _softmax_result(acc_ref).astype(o_ref.dtype)


def _flash(q, k, v, *, B, S, H, dk, dv, qcol, kcol, vcol, scale, cum_col=None, cum_row=None):
    N = B * S
    tq = _pick(S, 512, 8)
    tk = _pick(S, 512, LANES)
    rc = _pick(tq, 64, 8)
    nq = S // tq
    assert tq == tk and dv == LANES
    decay = cum_col is not None
    in_specs = [
        pl.BlockSpec((tq, dk), lambda b, h, i: (b * nq + i, qcol + h)),
        pl.BlockSpec((S, dk), lambda b, h, i: (b, kcol + h)),
        pl.BlockSpec((S, dv), lambda b, h, i: (b, vcol + h)),
    ]
    args = [q, k, v]
    if decay:
        in_specs += [
            pl.BlockSpec((tq, LANES), lambda b, h, i: (b * nq + i, h)),
            pl.BlockSpec((1, 1, 1, S), lambda b, h, i: (b, h, 0, 0)),
        ]
        args += [cum_col, cum_row]
    kern = functools.partial(_flash_kernel, tq=tq, tk=tk, rc=rc, scale=scale, decay=decay)
    return pl.pallas_call(
        kern,
        grid=(B, H, nq),
        in_specs=in_specs,
        out_specs=pl.BlockSpec((tq, dv), lambda b, h, i: (b * nq + i, h)),
        out_shape=jax.ShapeDtypeStruct((N, H * dv), f32),
        scratch_shapes=[
            pltpu.VMEM((2, tq, tk), f32),
            pltpu.VMEM((tq, tk), _MXU_DTYPE),
            pltpu.VMEM((tq, LANES), f32),
            pltpu.VMEM((tq, LANES), f32),
            pltpu.VMEM((tq, 2 * LANES), f32),
        ],
        compiler_params=_cparams(("arbitrary", "arbitrary", "arbitrary")),
        name="flash_fox" if decay else "flash_mla",
    )(*args)


def _fox_prep_kernel(ff_ref, b_ref, o_ref, rep_ref, *, chunk):
    S = ff_ref.shape[0]
    r = lax.broadcasted_iota(jnp.int32, (chunk, chunk), 0)
    c = lax.broadcasted_iota(jnp.int32, (chunk, chunk), 1)
    tri = (c <= r).astype(_MXU_DTYPE)

    def body(j, carry):
        r0 = pl.multiple_of(j * chunk, chunk)
        x = ff_ref[pl.ds(r0, chunk), :] + b_ref[...]
        lf = jnp.minimum(x, 0.0) - jnp.log(1.0 + jnp.exp(-jnp.abs(x)))
        hi, mid, lo = _split3(lf)
        cum = (_mm(tri, hi) + _mm(tri, mid)) + _mm(tri, lo) + carry
        o_ref[pl.ds(r0, chunk), :] = cum
        for h in range(FOX_HEADS):
            rep_ref[pl.ds(r0, chunk), h * LANES:(h + 1) * LANES] = jnp.broadcast_to(cum[:, h:h + 1], (chunk, LANES))
        return cum[chunk - 1:chunk, :]

    lax.fori_loop(0, S // chunk, body, jnp.zeros((1, LANES), f32))


def _fox_prep(p, b_pad, B, S):
    N = B * S
    kern = functools.partial(_fox_prep_kernel, chunk=_pick(S, 128, 8))
    return pl.pallas_call(
        kern,
        grid=(B,),
        in_specs=[
            pl.BlockSpec((S, LANES), lambda b: (b, (COL_FF - N_COLS16) // LANES)),
            pl.BlockSpec((1, LANES), lambda b: (0, 0)),
        ],
        out_specs=[pl.BlockSpec((S, LANES), lambda b: (b, 0)),
                   pl.BlockSpec((S, FOX_HEADS * LANES), lambda b: (b, 0))],
        out_shape=[jax.ShapeDtypeStruct((N, LANES), f32),
                   jax.ShapeDtypeStruct((N, FOX_HEADS * LANES), f32)],
        compiler_params=_cparams(("arbitrary",)),
        name="fox_prep",
    )(p, b_pad)


def _nsa_prep_kernel(q_ref, kc_ref, vc_ref, ks_ref, vs_ref, kw_ref, vw_ref, pos_ref, invf_ref, sign_ref, blk_ref,
                     qo_ref, cmp_ref, kso_ref, vso_ref, kwo_ref, vwo_ref, kc_scr, *, scale):
    cos_t, sin_t = _rot_tables(pos_ref, invf_ref, sign_ref)
    half = PARTIAL_ROT // 2
    rot = lambda x: _rotate(x.astype(f32), cos_t, sin_t, half, LANES)
    for h in range(NSA_HEADS):
        sl = slice(h * HEAD_DIM, (h + 1) * HEAD_DIM)
        qo_ref[:, sl] = (rot(q_ref[:, sl]) * scale).astype(qo_ref.dtype)
    nrow = cmp_ref.shape[2]
    for g in range(NSA_KV_GROUPS):
        sl = slice(g * HEAD_DIM, (g + 1) * HEAD_DIM)
        kso_ref[:, 2 * g * HEAD_DIM:(2 * g + 1) * HEAD_DIM] = rot(ks_ref[:, sl]).astype(kso_ref.dtype)
        kso_ref[:, (2 * g + 1) * HEAD_DIM:(2 * g + 2) * HEAD_DIM] = blk_ref[...]
        kwo_ref[:, sl] = rot(kw_ref[:, sl]).astype(kwo_ref.dtype)
        kc_scr[0] = rot(kc_ref[:, sl])
        kc_scr[1] = vc_ref[:, sl].astype(f32)
        for t in range(NSA_CMP_STRIDE):
            cs = slice(t * HEAD_DIM, (t + 1) * HEAD_DIM)
            for kv in range(2):
                cmp_ref[kv, g, :, cs] = kc_scr[kv, pl.ds(t, nrow, stride=NSA_CMP_STRIDE), :].astype(cmp_ref.dtype)
    vso_ref[...] = vs_ref[...].astype(vso_ref.dtype)
    vwo_ref[...] = vw_ref[...].astype(vwo_ref.dtype)


def _nsa_prep(p, p32, pos, invf, sign, expand, S):
    N = p.shape[0]
    tm = _pick(S, 512, 8)
    nt = S // tm
    G = NSA_KV_GROUPS
    cw = NSA_CMP_STRIDE * HEAD_DIM
    kv = lambda col: pl.BlockSpec((tm, NSA_KV), lambda i: (i, col // NSA_KV))
    row = lambda i: (0, 0)
    kern = functools.partial(_nsa_prep_kernel, scale=HEAD_DIM ** -0.5)
    return pl.pallas_call(
        kern,
        grid=(N // tm,),
        in_specs=[
            pl.BlockSpec((tm, W_NSA), lambda i: (i, COL_NQ // W_NSA)),
            kv(COL_NKC), kv(COL_NVC), kv(COL_NKS), kv(COL_NVS), kv(COL_NKW), kv(COL_NVW - N_COLS16),
            pl.BlockSpec((tm, 1), lambda i: (i, 0)),
            pl.BlockSpec((1, LANES), row),
            pl.BlockSpec((1, LANES), row),
            pl.BlockSpec((tm, LANES), lambda i: (i % nt, 0)),
        ],
        out_specs=[
            pl.BlockSpec((tm, W_NSA), lambda i: (i, 0)),
            pl.BlockSpec((2, G, tm // NSA_CMP_STRIDE, cw), lambda i: (0, 0, i, 0)),
            pl.BlockSpec((tm, 2 * NSA_KV), lambda i: (i, 0)),
            pl.BlockSpec((tm, NSA_KV), lambda i: (i, 0)),
            pl.BlockSpec((tm, NSA_KV), lambda i: (i, 0)),
            pl.BlockSpec((tm, NSA_KV), lambda i: (i, 0)),
        ],
        out_shape=[
            jax.ShapeDtypeStruct((N, W_NSA), _MXU_DTYPE),
            jax.ShapeDtypeStruct((2, G, N // NSA_CMP_STRIDE, cw), _MXU_DTYPE),
            jax.ShapeDtypeStruct((N, 2 * NSA_KV), _MXU_DTYPE),
            jax.ShapeDtypeStruct((N, NSA_KV), _MXU_DTYPE),
            jax.ShapeDtypeStruct((N, NSA_KV), _MXU_DTYPE),
            jax.ShapeDtypeStruct((N, NSA_KV), _MXU_DTYPE),
        ],
        scratch_shapes=[pltpu.VMEM((2, tm, HEAD_DIM), f32)],
        compiler_params=_cparams(("arbitrary",)),
        name="nsa_prep",
    )(p, p, p, p, p, p, p32, pos, invf, sign, expand)


def _compress_kernel(c_ref, pos_ref, w1_ref, w2_ref, o_ref):
    x = c_ref[0, 0, 0]
    half = x.shape[1]
    nrow = x.shape[0]
    a = _mm(x, w1_ref[0, :half, :])
    b = _mm(x, w1_ref[0, half:, :])
    posb = _mm(pos_ref[0].astype(_MXU_DTYPE), w1_ref[0])[0:1, :]
    hid = a + pltpu.roll(b, nrow - 1, 0) + posb
    c0 = math.sqrt(2.0 / math.pi)
    act = 0.5 * hid * (1.0 + jnp.tanh(c0 * (hid + 0.044715 * (hid * hid * hid))))
    o_ref[0, 0, 0] = _mm(act.astype(_MXU_DTYPE), w2_ref[0])


def _compress(cmp_in, pos_flat, w1, w2, B, S):
    G = NSA_KV_GROUPS
    nch = S // NSA_CMP_STRIDE
    width = NSA_CMP_STRIDE * HEAD_DIM
    x = cmp_in.reshape(2, G, B, nch, width)
    return pl.pallas_call(
        _compress_kernel,
        grid=(2, G, B),
        in_specs=[
            pl.BlockSpec((1, 1, 1, nch, width), lambda t, g, b: (t, g, b, 0, 0)),
            pl.BlockSpec((1, 8, 2 * width), lambda t, g, b: (t, 0, 0)),
            pl.BlockSpec((1, 2 * width, NSA_CMP_HIDDEN), lambda t, g, b: (t, 0, 0)),
            pl.BlockSpec((1, NSA_CMP_HIDDEN, HEAD_DIM), lambda t, g, b: (t, 0, 0)),
        ],
        out_specs=pl.BlockSpec((1, 1, 1, nch, HEAD_DIM), lambda t, g, b: (t, g, b, 0, 0)),
        out_shape=jax.ShapeDtypeStruct((2, G, B, nch, HEAD_DIM), f32),
        compiler_params=_cparams(("arbitrary", "arbitrary", "arbitrary")),
        name="nsa_compress",
    )(x, pos_flat, w1, w2)


def _nsa_attn_kernel(q_ref, kc_ref, vc_ref, ks_ref, vs_ref, kw_ref, vw_ref, g_ref, ovl_ref, o_ref,
                     s_ref, p_ref, m_ref, a_ref, acc_ref, out_scr, psum_ref,
                     *, tq, tks, tkw, rc, n_slc, top_k):
    i = pl.program_id(2)
    Hg = NSA_GQA
    D = HEAD_DIM
    q4 = q_ref[...]
    qs = jnp.concatenate([q4[:, h * D:(h + 1) * D] for h in range(Hg)], axis=0)
    t = i * tq + lax.broadcasted_iota(jnp.int32, (tq, 1), 0)
    gate = jax.nn.sigmoid(g_ref[...])
    head_rows = lambda h: slice(h * tq, (h + 1) * tq)

    kc = kc_ref[0, 0, 0].astype(_MXU_DTYPE)
    vc = vc_ref[0, 0, 0].astype(_MXU_DTYPE)
    ncmp = kc.shape[0]
    cend = lax.broadcasted_iota(jnp.int32, (1, ncmp), 1) * NSA_CMP_STRIDE + (NSA_CMP_LEN - 1)
    s_ref[0, :, :ncmp] = _mm_nt(qs, kc)
    cq = LANES
    for h in range(Hg):
        for q0 in range(0, tq, cq):
            rs = slice(h * tq + q0, h * tq + q0 + cq)
            tc = t[q0:q0 + cq]
            s = jnp.where(cend <= tc, s_ref[0, rs, :ncmp], NEG)
            e = jnp.exp(s - jnp.max(s, axis=1, keepdims=True))
            l = jnp.sum(e, axis=1, keepdims=True)
            p = e * jnp.where(tc >= NSA_CMP_LEN - 1, 1.0 / l, 0.0)
            p_ref[rs, :ncmp] = p.astype(p_ref.dtype)
            if h == 0:
                psum_ref[q0:q0 + cq] = p
            else:
                psum_ref[q0:q0 + cq] += p
    o_cmp = _mm(p_ref[:, :ncmp], vc)
    for h in range(Hg):
        out_scr[head_rows(h)] = gate[:, 3 * h:3 * h + 1] * o_cmp[head_rows(h)]

    hi, mid, lo = _split3(psum_ref[...])
    ovl = ovl_ref[...]
    imp = (_mm(hi, ovl) + _mm(mid, ovl)) + _mm(lo, ovl)
    nb = -(-n_slc // 8) * 8
    x = imp.T[:nb]
    jr = lax.broadcasted_iota(jnp.int32, (nb, 1), 0)
    cur_l = (i * tq + lax.broadcasted_iota(jnp.int32, (1, tq), 1)) // NSA_SLC_BLOCK
    forced = (jr == 0) | (jr == cur_l) | (jr == cur_l - 1)
    x = jnp.where(forced, BIG, jnp.where(jr > cur_l, NEG, x))
    xg = [x[g0:g0 + 8] for g0 in range(0, nb, 8)]
    jg = lax.broadcasted_iota(jnp.int32, (8, 1), 0)
    rg = [jnp.zeros((8, tq), f32) for _ in xg]
    for kk in range(n_slc):
        ck = jnp.broadcast_to(x[kk:kk + 1, :], (8, tq))
        for g, xv in enumerate(xg):
            if 8 * g > kk:
                before = ck >= xv
            elif 8 * g + 7 < kk:
                before = ck > xv
            else:
                before = (ck > xv) | ((ck == xv) & (jg + 8 * g > kk))
            rg[g] = rg[g] + before.astype(f32)
    rank = jnp.concatenate(rg, axis=0)
    unsel_t = jnp.where((rank < float(top_k)) & (jr <= cur_l), 0.0, NEG)
    if nb < LANES:
        unsel_t = jnp.concatenate([unsel_t, jnp.zeros((LANES - nb, tq), f32)], axis=0)
    unsel = unsel_t.T.astype(_MXU_DTYPE)
    q_aug = jnp.concatenate([qs, jnp.concatenate([unsel] * Hg, axis=0)], axis=1)

    def store_masked(slot, s, allowed, width):
        bias = jnp.where(allowed, 0.0, NEG)
        for h in range(Hg):
            s_ref[slot, head_rows(h), :width] = s[head_rows(h)] + bias

    def add_gated(c):
        o = _softmax_result(acc_ref)
        for h in range(Hg):
            out_scr[head_rows(h)] += gate[:, 3 * h + c:3 * h + c + 1] * o[head_rows(h)]

    n_ts = ((i + 1) * tq + tks - 1) // tks

    def slc_produce(kt, slot):
        k0 = pl.multiple_of(kt * tks, tks)
        s_ref[slot] = _mm_nt(q_aug, ks_ref[pl.ds(k0, tks), :])

    def slc_mask_last(slot):
        col = (n_ts - 1) * tks + lax.broadcasted_iota(jnp.int32, (1, tks), 1)
        for r0 in range(0, Hg * tq, rc):
            q0 = r0 % tq
            s_ref[slot, r0:r0 + rc, :] = jnp.where(col <= t[q0:q0 + rc], s_ref[slot, r0:r0 + rc, :], NEG)

    def slc_consume(kt, slot):
        k0 = pl.multiple_of(kt * tks, tks)
        _softmax_step(s_ref, slot, tks, vs_ref[pl.ds(k0, tks), :], m_ref, a_ref, p_ref, acc_ref, rc)

    _softmax_init(m_ref, acc_ref)
    _tile_pipeline(n_ts, slc_produce, slc_consume, slc_mask_last)
    add_gated(1)

    hi_t = (i * tq + tq - 1) // tkw
    lo_t = jnp.maximum(i * tq - (NSA_WINDOW - 1), 0) // tkw

    def win_produce(j, slot):
        k0 = pl.multiple_of((hi_t - j) * tkw, tkw)
        col = k0 + lax.broadcasted_iota(jnp.int32, (1, tkw), 1)
        store_masked(slot, _mm_nt(qs, kw_ref[pl.ds(k0, tkw), :]), (col <= t) & (t - col < NSA_WINDOW), tkw)

    def win_consume(j, slot):
        k0 = pl.multiple_of((hi_t - j) * tkw, tkw)
        _softmax_step(s_ref, slot, tkw, vw_ref[pl.ds(k0, tkw), :], m_ref, a_ref, p_ref, acc_ref,
                      rc * (tks // tkw))

    _softmax_init(m_ref, acc_ref)
    _tile_pipeline(hi_t - lo_t + 1, win_produce, win_consume)
    add_gated(2)
    for h in range(Hg):
        o_ref[:, h * D:(h + 1) * D] = out_scr[head_rows(h)].astype(o_ref.dtype)


def _nsa_attn(qn, cmp_kv, ks, vs, kw, vw, p, ovl, B, S):
    N = B * S
    G = NSA_KV_GROUPS
    D = HEAD_DIM
    tq = _pick(S, 256, LANES)
    tks = _pick(S, 512, LANES)
    tkw = _pick(S, 256, LANES)
    rc = _pick(tq, 64, 8)
    nq = S // tq
    nch = cmp_kv.shape[3]
    n_slc = S // NSA_SLC_BLOCK
    assert n_slc <= LANES and nch <= tks and tkw <= tks
    kern = functools.partial(_nsa_attn_kernel, tq=tq, tks=tks, tkw=tkw, rc=rc, n_slc=n_slc,
                             top_k=min(NSA_TOPK, n_slc))
    R = NSA_GQA * tq
    tk = tks
    seq = lambda: pl.BlockSpec((S, D), lambda b, g, i: (b, g))
    return pl.pallas_call(
        kern,
        grid=(B, G, nq),
        in_specs=[
            pl.BlockSpec((tq, NSA_GQA * D), lambda b, g, i: (b * nq + i, g)),
            pl.BlockSpec((1, 1, 1, nch, D), lambda b, g, i: (0, g, b, 0, 0)),
            pl.BlockSpec((1, 1, 1, nch, D), lambda b, g, i: (1, g, b, 0, 0)),
            pl.BlockSpec((S, 2 * D), lambda b, g, i: (b, g)),
            seq(), seq(), seq(),
            pl.BlockSpec((tq, LANES), lambda b, g, i: (b * nq + i, (COL_NG - N_COLS16) // LANES + g)),
            pl.BlockSpec(ovl.shape, lambda b, g, i: (0, 0)),
        ],
        out_specs=pl.BlockSpec((tq, NSA_GQA * D), lambda b, g, i: (b * nq + i, g)),
        out_shape=jax.ShapeDtypeStruct((N, W_NSA), f32),
        scratch_shapes=[
            pltpu.VMEM((2, R, tk), f32),
            pltpu.VMEM((R, tk), _MXU_DTYPE),
            pltpu.VMEM((R, LANES), f32),
            pltpu.VMEM((R, LANES), f32),
            pltpu.VMEM((R, 2 * D), f32),
            pltpu.VMEM((R, D), f32),
            pltpu.VMEM((tq, nch), f32),
        ],
        compiler_params=_cparams(("arbitrary", "arbitrary", "arbitrary")),
        name="nsa_attn",
    )(qn, cmp_kv, cmp_kv, ks, vs, kw, vw, p, ovl)


def _nsa_consts(S):
    nch = S // NSA_CMP_STRIDE
    n_slc = S // NSA_SLC_BLOCK
    c_start = np.arange(nch) * NSA_CMP_STRIDE
    s_start = np.arange(n_slc) * NSA_SLC_BLOCK
    ov = (np.minimum(c_start[:, None] + NSA_CMP_LEN, s_start[None, :] + NSA_SLC_BLOCK)
          - np.maximum(c_start[:, None], s_start[None, :]))
    ovl = np.zeros((nch, LANES), np.float32)
    ovl[:, :n_slc] = np.clip(ov, 0, None) / NSA_CMP_LEN
    expand = np.zeros((S, LANES), np.float32)
    expand[np.arange(S), np.arange(S) // NSA_SLC_BLOCK] = 1.0
    return jnp.asarray(ovl, _MXU_DTYPE), jnp.asarray(expand, _MXU_DTYPE)


def _out_proj_kernel(om_ref, of_ref, on_ref, gm_ref, gf_ref, gn_ref, w_ref, x_ref, gt_ref, o_ref, h_ref):
    @pl.when(pl.program_id(1) == 0)
    def _():
        h_ref[:, :W_MLA] = _rms(om_ref[...], gm_ref[...]).astype(h_ref.dtype)
        h_ref[:, W_MLA:W_MLA + W_FOX] = _rms(of_ref[...], gf_ref[...]).astype(h_ref.dtype)
        h_ref[:, W_MLA + W_FOX:] = _rms(on_ref[...], gn_ref[...]).astype(h_ref.dtype)

    o_ref[...] = x_ref[...] + gt_ref[0] * _mm(h_ref[...], w_ref[0])


def _out_proj(o_mla, o_fox, o_nsa, gm, gf, gn, w, l, x2, mod3, S):
    N, D = x2.shape
    DM = w.shape[1]
    tm = _pick(S, 512, 8)
    tn = _pick(D, 2048)
    nj = D // tn
    row = lambda i, j: (0, 0)
    return pl.pallas_call(
        _out_proj_kernel,
        grid=(N // tm, nj),
        in_specs=[
            pl.BlockSpec((tm, W_MLA), lambda i, j: (i, 0)),
            pl.BlockSpec((tm, W_FOX), lambda i, j: (i, 0)),
            pl.BlockSpec((tm, W_NSA), lambda i, j: (i, 0)),
            pl.BlockSpec((1, W_MLA), row),
            pl.BlockSpec((1, W_FOX), row),
            pl.BlockSpec((1, W_NSA), row),
            pl.BlockSpec((1, DM, tn), lambda i, j: (l, 0, j)),
            pl.BlockSpec((tm, tn), lambda i, j: (i, j)),
            pl.BlockSpec((1, 1, tn), lambda i, j: (i * tm // S, 0, 2 * nj + j)),
        ],
        out_specs=pl.BlockSpec((tm, tn), lambda i, j: (i, j)),
        out_shape=jax.ShapeDtypeStruct((N, D), f32),
        scratch_shapes=[pltpu.VMEM((tm, DM), _MXU_DTYPE)],
        compiler_params=_cparams(("arbitrary", "arbitrary")),
        name="out_proj",
    )(o_mla, o_fox, o_nsa, gm, gf, gn, w, x2, mod3)


def _mlp_kernel(x_ref, sh_ref, sc_ref, gt_ref, g_ref, w1_ref, w2_ref, fg_ref, o_ref, h_ref, acc_ref, *, final):
    f = pl.program_id(1)

    @pl.when(f == 0)
    def _():
        h = _rms(x_ref[...], g_ref[...]) * (1.0 + sc_ref[0]) + sh_ref[0]
        h_ref[...] = h.astype(h_ref.dtype)
        acc_ref[...] = jnp.zeros_like(acc_ref)

    u = jnp.maximum(_mm(h_ref[...], w1_ref[0]), 0.0)
    acc_ref[...] += _mm((u * u).astype(_MXU_DTYPE), w2_ref[0])

    @pl.when(f == pl.num_programs(1) - 1)
    def _():
        y = x_ref[...] + gt_ref[0] * acc_ref[...]
        if final:
            y = _rms(y, fg_ref[...])
        o_ref[...] = y


def _mlp(x2, mod3, gain, w1, w2, l, final_gain, S, final):
    N, D = x2.shape
    F = w1.shape[2]
    tm = _pick(S, 512, 8)
    tf = _pick(F, 1024)
    row = lambda i, f: (0, 0)
    mod_blk = lambda c: pl.BlockSpec((1, 1, D), lambda i, f: (i * tm // S, 0, c))
    return pl.pallas_call(
        functools.partial(_mlp_kernel, final=final),
        grid=(N // tm, F // tf),
        in_specs=[
            pl.BlockSpec((tm, D), lambda i, f: (i, 0)),
            mod_blk(3), mod_blk(4), mod_blk(5),
            pl.BlockSpec((1, D), row),
            pl.BlockSpec((1, D, tf), lambda i, f: (l, 0, f)),
            pl.BlockSpec((1, tf, D), lambda i, f: (l, f, 0)),
            pl.BlockSpec((1, D), row),
        ],
        out_specs=pl.BlockSpec((tm, D), lambda i, f: (i, 0)),
        out_shape=jax.ShapeDtypeStruct((N, D), f32),
        scratch_shapes=[pltpu.VMEM((tm, D), _MXU_DTYPE), pltpu.VMEM((tm, D), f32)],
        compiler_params=_cparams(("arbitrary", "arbitrary")),
        name="mlp",
    )(x2, mod3, mod3, mod3, gain, w1, w2, final_gain)


def _w_in_segments():
    widths = (MLA_Q_LORA, MLA_KV_LORA, MLA_ROPE, W_FOX, W_FOX, W_FOX, FOX_HEADS,
              W_NSA, NSA_KV, NSA_KV, NSA_KV, NSA_KV, NSA_KV, NSA_KV, 3 * NSA_HEADS)
    src = np.cumsum((0,) + widths)
    (cq, ckv, kr, fq, fk, fv, ff, nq, nkc, nvc, nks, nvs, nkw, nvw, ng) = [int(s) for s in src[:-1]]
    gw = 3 * NSA_GQA
    return [(COL_NQ, nq, W_NSA), (COL_CQ, cq, MLA_Q_LORA), (COL_FQ, fq, W_FOX), (COL_FK, fk, W_FOX),
            (COL_FV, fv, W_FOX), (COL_CKV, ckv, MLA_KV_LORA), (COL_NKC, nkc, NSA_KV), (COL_NVC, nvc, NSA_KV),
            (COL_NKS, nks, NSA_KV), (COL_NVS, nvs, NSA_KV), (COL_NKW, nkw, NSA_KV), (COL_NVW, nvw, NSA_KV),
            (COL_KR, kr, MLA_ROPE), (COL_FF, ff, FOX_HEADS), (COL_NG, ng, gw), (COL_NG + LANES, ng + gw, gw)]


def _w_in_regroup_kernel(w_ref, o_ref):
    o_ref[...] = jnp.zeros(o_ref.shape, o_ref.dtype)
    for dst, src, width in _w_in_segments():
        o_ref[:, dst:dst + width] = w_ref[0, :, src:src + width].astype(o_ref.dtype)


def _prep_w_in(w_in, l):
    _, D, C = w_in.shape
    tr = _pick(D, 256, 8)
    return pl.pallas_call(
        _w_in_regroup_kernel,
        grid=(D // tr,),
        in_specs=[pl.BlockSpec((1, tr, C), lambda i: (l, i, 0))],
        out_specs=pl.BlockSpec((tr, N_COLS), lambda i: (i, 0)),
        out_shape=jax.ShapeDtypeStruct((D, N_COLS), _MXU_DTYPE),
        compiler_params=_cparams(("arbitrary",)),
        name="w_in_regroup",
    )(w_in)


def _prep_mla_w(w_uq, w_ukv):
    H = MLA_HEADS
    qd = MLA_NOPE + MLA_ROPE
    q_nope = np.concatenate([np.arange(h * qd, h * qd + MLA_NOPE) for h in range(H)])
    q_rope = np.concatenate([np.arange(h * qd + MLA_NOPE, (h + 1) * qd) for h in range(H)])
    kd = MLA_NOPE + MLA_V
    k_nope = np.concatenate([np.arange(h * kd, h * kd + MLA_NOPE) for h in range(H)])
    v_cols = np.concatenate([np.arange(h * kd + MLA_NOPE, (h + 1) * kd) for h in range(H)])
    wq = jnp.take(w_uq, jnp.asarray(np.concatenate([q_nope, q_rope])), axis=1).astype(_MXU_DTYPE)
    wkv = jnp.take(w_ukv, jnp.asarray(np.concatenate([k_nope, v_cols])), axis=1).astype(_MXU_DTYPE)
    return wq, wkv


def _pad_lanes(v):
    return jnp.zeros((1, LANES), f32).at[0, :v.shape[0]].set(v.astype(f32))


def kernel(x, c, positions, ada_w, ada_b, norm_mix, norm_mlp, w_in, fox_b_f, mla_q_norm, mla_w_uq, mla_kv_norm,
           mla_w_ukv, nsa_pos_k, nsa_pos_v, nsa_cmp_k_w1, nsa_cmp_k_w2, nsa_cmp_v_w1, nsa_cmp_v_w2, out_norm,
           w_out, mlp_w1, mlp_w2, final_norm):
    B, S, D = x.shape
    L = ada_w.shape[0]
    N = B * S
    assert S % LANES == 0 and D % LANES == 0 and B <= 8

    x2 = x.reshape(N, D)
    pos = positions.reshape(N, 1).astype(jnp.int32)
    c_pad = jnp.zeros((8, D), f32).at[:B].set(c)
    mod = _ada_mod(c_pad, ada_w, ada_b)

    invf_mla, sign_mla = _rope_consts(MLA_ROPE // 2, MLA_ROPE, LANES)
    invf_nsa, sign_nsa = _rope_consts(PARTIAL_ROT // 2, LANES, LANES)
    ovl, expand = _nsa_consts(S)
    row = lambda v: v.reshape(1, -1)
    w_out_c = w_out.astype(_MXU_DTYPE)
    w1_c = mlp_w1.astype(_MXU_DTYPE)
    w2_c = mlp_w2.astype(_MXU_DTYPE)

    for l in range(L):
        mod3 = mod[l, :B].reshape(B, 1, 6 * D)
        p, p32 = _proj_in(x2, mod3, row(norm_mix[l]), _prep_w_in(w_in, l), S)

        wq, wkv = _prep_mla_w(mla_w_uq[l], mla_w_ukv[l])
        q_m, k_m, v_m = _mla_up(p, p32, pos, invf_mla, sign_mla, row(mla_q_norm[l]), row(mla_kv_norm[l]),
                                wq, wkv, S)
        o_mla = _flash(q_m, k_m, v_m, B=B, S=S, H=MLA_HEADS, dk=MLA_QK, dv=MLA_V, qcol=0, kcol=0, vcol=0,
                       scale=1.0)

        cum, cum_rep = _fox_prep(p32, _pad_lanes(fox_b_f[l]), B, S)
        cum_row = cum[:, :FOX_HEADS].reshape(B, S, FOX_HEADS).transpose(0, 2, 1)[:, :, None, :]
        o_fox = _flash(p, p, p, B=B, S=S, H=FOX_HEADS, dk=HEAD_DIM, dv=HEAD_DIM,
                       qcol=COL_FQ // HEAD_DIM, kcol=COL_FK // HEAD_DIM, vcol=COL_FV // HEAD_DIM,
                       scale=HEAD_DIM ** -0.5, cum_col=cum_rep, cum_row=cum_row)

        qn, cmp_in, ks, vs, kw, vw = _nsa_prep(p, p32, pos, invf_nsa, sign_nsa, expand, S)
        flat = lambda a: jnp.broadcast_to(a.reshape(1, -1), (8, a.size))
        cmp_kv = _compress(
            cmp_in,
            jnp.stack([flat(nsa_pos_k[l]), flat(nsa_pos_v[l])]).astype(f32),
            jnp.stack([nsa_cmp_k_w1[l], nsa_cmp_v_w1[l]]).astype(_MXU_DTYPE),
            jnp.stack([nsa_cmp_k_w2[l], nsa_cmp_v_w2[l]]).astype(_MXU_DTYPE), B, S)
        o_nsa = _nsa_attn(qn, cmp_kv, ks, vs, kw, vw, p32, ovl, B, S)

        gn = out_norm[l]
        x2 = _out_proj(o_mla, o_fox, o_nsa, row(gn[:W_MLA]), row(gn[W_MLA:W_MLA + W_FOX]),
                       row(gn[W_MLA + W_FOX:]), w_out_c, l, x2, mod3, S)
        x2 = _mlp(x2, mod3, row(norm_mlp[l]), w1_c, w2_c, l, row(final_norm), S, final=(l == L - 1))
    return x2.reshape(B, S, D)
```

```python
import functools
import math

import numpy as np
import jax
import jax.numpy as jnp
from jax import lax
from jax.experimental import pallas as pl
from jax.experimental.pallas import tpu as pltpu

f32 = jnp.float32
_MXU_DTYPE = jnp.bfloat16

HEAD_DIM = 128
MLA_HEADS = 4
MLA_Q_LORA = 512
MLA_KV_LORA = 256
MLA_NOPE = 128
MLA_ROPE = 64
MLA_V = 128
FOX_HEADS = 4
NSA_HEADS = 8
NSA_KV_GROUPS = 2
NSA_GQA = NSA_HEADS // NSA_KV_GROUPS
NSA_CMP_LEN = 32
NSA_CMP_STRIDE = 16
NSA_CMP_HIDDEN = 512
NSA_SLC_BLOCK = 64
NSA_TOPK = 16
NSA_WINDOW = 512
W_MLA = MLA_HEADS * MLA_V
W_FOX = FOX_HEADS * HEAD_DIM
W_NSA = NSA_HEADS * HEAD_DIM
NSA_KV = NSA_KV_GROUPS * HEAD_DIM
ROPE_THETA = 500000.0
PARTIAL_ROT = HEAD_DIM // 4
EPS = 1e-6
NEG = -1e30
BIG = 1e30

LANES = 128
VMEM_LIMIT = 56 * 1024 * 1024

MLA_QK = 2 * LANES
COL_NQ = 0
COL_CQ = 1024
COL_FQ = 1536
COL_FK = 2048
COL_FV = 2560
COL_CKV = 3072
COL_NKC = 3328
COL_NVC = 3584
COL_NKS = 3840
COL_NVS = 4096
COL_NKW = 4352
COL_NVW = 4608
COL_KR = 4864
COL_FF = 4992
COL_NG = 5120
N_COLS = 5376
N_COLS16 = COL_NVW


def _pick(n, cap, mult=LANES):
    if n <= cap:
        return n
    t = (cap // mult) * mult
    while t >= mult:
        if n % t == 0:
            return t
        t -= mult
    raise ValueError(f"no tile for {n} under {cap}")


def _mm(a, b):
    return jnp.dot(a, b, preferred_element_type=f32)


def _mm_nt(a, b):
    return lax.dot_general(a, b, (((1,), (1,)), ((), ())), preferred_element_type=f32)


def _split3(a):
    hi = a.astype(_MXU_DTYPE)
    r1 = a - hi.astype(f32)
    mid = r1.astype(_MXU_DTYPE)
    lo = (r1 - mid.astype(f32)).astype(_MXU_DTYPE)
    return hi, mid, lo


def _rms(x, gain):
    return x * lax.rsqrt(jnp.mean(x * x, axis=-1, keepdims=True) + EPS) * gain


def _cparams(sem):
    return pltpu.CompilerParams(dimension_semantics=sem, vmem_limit_bytes=VMEM_LIMIT)


def _ada_kernel(c_ref, w_ref, b_ref, o_ref):
    c = c_ref[...]
    ca = c * jax.nn.sigmoid(c)
    w = w_ref[0]
    c_hi, c_mid, _ = _split3(ca)
    w_hi, w_mid, _ = _split3(w)
    acc = _mm(c_hi, w_hi) + (_mm(c_mid, w_hi) + _mm(c_hi, w_mid))
    o_ref[0] = acc + b_ref[0]


def _ada_mod(c_pad, ada_w, ada_b):
    L, D, D6 = ada_w.shape
    tn = _pick(D6, 1024)
    return pl.pallas_call(
        _ada_kernel,
        grid=(L, D6 // tn),
        in_specs=[
            pl.BlockSpec((8, D), lambda l, j: (0, 0)),
            pl.BlockSpec((1, D, tn), lambda l, j: (l, 0, j)),
            pl.BlockSpec((1, 1, tn), lambda l, j: (l, 0, j)),
        ],
        out_specs=pl.BlockSpec((1, 8, tn), lambda l, j: (l, 0, j)),
        out_shape=jax.ShapeDtypeStruct((L, 8, D6), f32),
        compiler_params=_cparams(("arbitrary", "arbitrary")),
        name="ada_mod",
    )(c_pad, ada_w, ada_b.reshape(L, 1, D6))


def _proj_in_kernel(x_ref, sh_ref, sc_ref, g_ref, w_ref, o16_ref, o32_ref, h_ref, *, n16):
    j = pl.program_id(1)

    @pl.when(j == 0)
    def _():
        h = _rms(x_ref[...], g_ref[...]) * (1.0 + sc_ref[0]) + sh_ref[0]
        h_ref[...] = h.astype(h_ref.dtype)

    r = _mm(h_ref[...], w_ref[...])

    @pl.when(j < n16)
    def _():
        o16_ref[...] = r.astype(o16_ref.dtype)

    @pl.when(j >= n16)
    def _():
        o32_ref[...] = r


def _proj_in(x2, mod3, gain, w, S):
    N, D = x2.shape
    NC = w.shape[1]
    tm = _pick(S, 1024, 8)
    tn = N_COLS - N_COLS16
    n16 = N_COLS16 // tn
    assert NC == N_COLS and N_COLS16 % tn == 0
    return pl.pallas_call(
        functools.partial(_proj_in_kernel, n16=n16),
        grid=(N // tm, NC // tn),
        in_specs=[
            pl.BlockSpec((tm, D), lambda i, j: (i, 0)),
            pl.BlockSpec((1, 1, D), lambda i, j: (i * tm // S, 0, 0)),
            pl.BlockSpec((1, 1, D), lambda i, j: (i * tm // S, 0, 1)),
            pl.BlockSpec((1, D), lambda i, j: (0, 0)),
            pl.BlockSpec((D, tn), lambda i, j: (0, j)),
        ],
        out_specs=[pl.BlockSpec((tm, tn), lambda i, j: (i, jnp.minimum(j, n16 - 1))),
                   pl.BlockSpec((tm, tn), lambda i, j: (i, 0))],
        out_shape=[jax.ShapeDtypeStruct((N, N_COLS16), _MXU_DTYPE),
                   jax.ShapeDtypeStruct((N, tn), f32)],
        scratch_shapes=[pltpu.VMEM((tm, D), _MXU_DTYPE)],
        compiler_params=_cparams(("arbitrary", "arbitrary")),
        name="proj_in",
    )(x2, mod3, mod3, gain, w)


def _rot_tables(pos_ref, invf_ref, sign_ref):
    ang = pos_ref[...].astype(f32) * invf_ref[...]
    return jnp.cos(ang), jnp.sin(ang) * sign_ref[...]


def _rotate(x, cos_t, sin_t, half, period):
    lane = lax.broadcasted_iota(jnp.int32, x.shape, 1)
    partner = jnp.where((lane % period) < half,
                        pltpu.roll(x, LANES - half, 1),
                        pltpu.roll(x, half, 1))
    return x * cos_t + partner * sin_t


def _rope_consts(half, period, width):
    inv = ROPE_THETA ** (-jnp.arange(half, dtype=f32) / half)
    lane = np.arange(LANES)
    in_rot = ((lane % period) < 2 * half) & (lane < width)
    idx = (lane % period) % half
    invf = jnp.where(jnp.asarray(in_rot), inv[idx], 0.0).reshape(1, LANES).astype(f32)
    sign = np.where(in_rot, np.where((lane % period) < half, -1.0, 1.0), 0.0)
    return invf, jnp.asarray(sign, f32).reshape(1, LANES)


def _mla_up_kernel(cq_ref, ckv_ref, kr_ref, pos_ref, invf_ref, sign_ref, gq_ref, gkv_ref, wq_ref, wkv_ref,
                   q_ref, k_ref, v_ref, *, scale):
    cos_t, sin_t = _rot_tables(pos_ref, invf_ref, sign_ref)
    half = MLA_ROPE // 2
    qn = _rms(cq_ref[...].astype(f32), gq_ref[...]).astype(_MXU_DTYPE)
    q = _mm(qn, wq_ref[...]) * scale
    kvn = _rms(ckv_ref[...].astype(f32), gkv_ref[...]).astype(_MXU_DTYPE)
    kv = _mm(kvn, wkv_ref[...])
    k_rope = _rotate(kr_ref[...], cos_t, sin_t, half, MLA_ROPE).astype(k_ref.dtype)
    lane = lax.broadcasted_iota(jnp.int32, (q.shape[0], LANES), 1)
    H = MLA_HEADS
    for pair in range(H // 2):
        c0 = H * MLA_NOPE + pair * LANES
        r = _rotate(q[:, c0:c0 + LANES], cos_t, sin_t, half, MLA_ROPE)
        r_even = jnp.where(lane < MLA_ROPE, r, 0.0)
        r_odd = jnp.where(lane < MLA_ROPE, pltpu.roll(r, MLA_ROPE, 1), 0.0)
        for h, rr in ((2 * pair, r_even), (2 * pair + 1, r_odd)):
            q_ref[:, h * MLA_QK:h * MLA_QK + LANES] = q[:, h * MLA_NOPE:(h + 1) * MLA_NOPE].astype(q_ref.dtype)
            q_ref[:, h * MLA_QK + LANES:(h + 1) * MLA_QK] = rr.astype(q_ref.dtype)
    for h in range(H):
        k_ref[:, h * MLA_QK:h * MLA_QK + LANES] = kv[:, h * MLA_NOPE:(h + 1) * MLA_NOPE].astype(k_ref.dtype)
        k_ref[:, h * MLA_QK + LANES:(h + 1) * MLA_QK] = k_rope
    v_ref[...] = kv[:, H * MLA_NOPE:].astype(v_ref.dtype)


def _mla_up(p, p32, pos, invf, sign, gq, gkv, wq, wkv, S):
    N = p.shape[0]
    tm = _pick(S, 512, 8)
    H = MLA_HEADS
    kern = functools.partial(_mla_up_kernel, scale=(MLA_NOPE + MLA_ROPE) ** -0.5)
    row = lambda i: (0, 0)
    return pl.pallas_call(
        kern,
        grid=(N // tm,),
        in_specs=[
            pl.BlockSpec((tm, MLA_Q_LORA), lambda i: (i, COL_CQ // MLA_Q_LORA)),
            pl.BlockSpec((tm, MLA_KV_LORA), lambda i: (i, COL_CKV // MLA_KV_LORA)),
            pl.BlockSpec((tm, LANES), lambda i: (i, (COL_KR - N_COLS16) // LANES)),
            pl.BlockSpec((tm, 1), lambda i: (i, 0)),
            pl.BlockSpec((1, LANES), row),
            pl.BlockSpec((1, LANES), row),
            pl.BlockSpec((1, MLA_Q_LORA), row),
            pl.BlockSpec((1, MLA_KV_LORA), row),
            pl.BlockSpec(wq.shape, row),
            pl.BlockSpec(wkv.shape, row),
        ],
        out_specs=[
            pl.BlockSpec((tm, H * MLA_QK), lambda i: (i, 0)),
            pl.BlockSpec((tm, H * MLA_QK), lambda i: (i, 0)),
            pl.BlockSpec((tm, H * MLA_V), lambda i: (i, 0)),
        ],
        out_shape=[
            jax.ShapeDtypeStruct((N, H * MLA_QK), _MXU_DTYPE),
            jax.ShapeDtypeStruct((N, H * MLA_QK), _MXU_DTYPE),
            jax.ShapeDtypeStruct((N, H * MLA_V), _MXU_DTYPE),
        ],
        compiler_params=_cparams(("arbitrary",)),
        name="mla_up",
    )(p, p, p32, pos, invf, sign, gq, gkv, wq, wkv)


def _softmax_init(m_ref, acc_ref):
    m_ref[...] = jnp.full(m_ref.shape, NEG, f32)
    acc_ref[...] = jnp.zeros(acc_ref.shape, f32)


def _softmax_step(s_ref, slot, tkc, v, m_ref, a_ref, p_ref, acc_ref, rc):
    rows = m_ref.shape[0]
    nc = tkc // LANES
    col = lambda rs, c: s_ref[slot, rs, c * LANES:(c + 1) * LANES]
    for r0 in range(0, rows, rc):
        rs = slice(r0, r0 + rc)
        m = m_ref[rs]
        m_new = jnp.maximum(m, jnp.max(functools.reduce(jnp.maximum, [col(rs, c) for c in range(nc)]),
                                       axis=1, keepdims=True))
        m_ref[rs] = m_new
        a_ref[rs] = jnp.exp(m - m_new)
    for r0 in range(0, rows, rc):
        rs = slice(r0, r0 + rc)
        m_new = m_ref[rs]
        psum = None
        for c in range(nc):
            pc = jnp.exp(col(rs, c) - m_new)
            p_ref[rs, c * LANES:(c + 1) * LANES] = pc.astype(p_ref.dtype)
            psum = pc if psum is None else psum + pc
        acc_ref[rs, LANES:] = a_ref[rs] * acc_ref[rs, LANES:] + jnp.sum(psum, axis=1, keepdims=True)
    vs = v if isinstance(v, (list, tuple)) else [v]
    rv = rows // len(vs)
    for n, vn in enumerate(vs):
        rs = slice(n * rv, (n + 1) * rv)
        acc_ref[rs, :LANES] = a_ref[rs] * acc_ref[rs, :LANES] + _mm(p_ref[rs, :tkc], vn)


def _softmax_result(acc_ref):
    return acc_ref[:, :LANES] / acc_ref[:, LANES:]


def _tile_pipeline(n, produce, consume, finish_last=None):
    finish = (lambda slot: None) if finish_last is None else finish_last
    odd = n % 2

    @pl.when(n == 1)
    def _():
        produce(0, 0)
        finish(0)
        consume(0, 0)

    @pl.when(n >= 2)
    def _():
        produce(0, 0)

        def pair(jj, carry):
            j = 2 * jj
            produce(j + 1, 1)
            consume(j, 0)
            produce(j + 2, 0)
            consume(j + 1, 1)
            return carry

        lax.fori_loop(0, (n - 2 - odd) // 2, pair, 0)

        @pl.when(odd == 0)
        def _():
            produce(n - 1, 1)
            finish(1)
            consume(n - 2, 0)
            consume(n - 1, 1)

        @pl.when(odd == 1)
        def _():
            produce(n - 2, 1)
            consume(n - 3, 0)
            produce(n - 1, 0)
            finish(0)
            consume(n - 2, 1)
            consume(n - 1, 0)


def _flash_kernel(*refs, tq, tk, rc, scale, decay, hpb, dk, dv):
    if decay:
        q_ref, k_ref, v_ref, cq_ref, ck_ref, o_ref, s_ref, p_ref, m_ref, a_ref, acc_ref = refs
    else:
        q_ref, k_ref, v_ref, o_ref, s_ref, p_ref, m_ref, a_ref, acc_ref = refs
    i = pl.program_id(2)
    heads = range(hpb)
    rows = lambda h: slice(h * tq, (h + 1) * tq)
    qs = []
    for h in heads:
        q = q_ref[:, h * dk:(h + 1) * dk]
        if scale != 1.0:
            q = q.astype(f32) * scale
        qs.append(q.astype(_MXU_DTYPE))
    if decay:
        cqs = [cq_ref[:, h * LANES:(h + 1) * LANES] for h in heads]

    def produce(kt, slot):
        k0 = pl.multiple_of(kt * tk, tk)
        for h in heads:
            s = _mm_nt(qs[h], k_ref[pl.ds(k0, tk), h * dk:(h + 1) * dk].astype(_MXU_DTYPE))
            if decay:
                ck = ck_ref[0, h, :, pl.ds(k0, tk)]
                for c in range(tk // LANES):
                    cs = slice(c * LANES, (c + 1) * LANES)
                    s_ref[slot, rows(h), cs] = (s[:, cs] - ck[:, cs]) + cqs[h]
            else:
                s_ref[slot, rows(h), :] = s

    def mask_diagonal(slot):
        col = lax.broadcasted_iota(jnp.int32, (1, tk), 1)
        for r0 in range(0, hpb * tq, rc):
            row = r0 % tq + lax.broadcasted_iota(jnp.int32, (rc, 1), 0)
            s_ref[slot, r0:r0 + rc, :] = jnp.where(col <= row, s_ref[slot, r0:r0 + rc, :], NEG)

    def consume(kt, slot):
        k0 = pl.multiple_of(kt * tk, tk)
        vs = [v_ref[pl.ds(k0, tk), h * dv:(h + 1) * dv].astype(_MXU_DTYPE) for h in heads]
        _softmax_step(s_ref, slot, tk, vs, m_ref, a_ref, p_ref, acc_ref, rc)

    _softmax_init(m_ref, acc_ref)
    _tile_pipeline(i + 1, produce, consume, mask_diagonal)
    o = _softmax_result(acc_ref)
    for h in heads:
        o_ref[:, h * dv:(h + 1) * dv] = o[rows(h)].astype(o_ref.dtype)


def _flash(q, k, v, *, B, S, H, dk, dv, qcol, kcol, vcol, scale, cum_col=None, cum_row=None):
    N = B * S
    tq = _pick(S, 512, 8)
    tk = _pick(S, 512, LANES)
    rc = _pick(tq, 64, 8)
    nq = S // tq
    decay = cum_col is not None
    hpb = 2 if decay else 1
    assert tq == tk and dv == LANES and H % hpb == 0 and qcol % hpb == 0 and kcol % hpb == 0 and vcol % hpb == 0
    in_specs = [
        pl.BlockSpec((tq, hpb * dk), lambda b, h, i: (b * nq + i, qcol // hpb + h)),
        pl.BlockSpec((S, hpb * dk), lambda b, h, i: (b, kcol // hpb + h)),
        pl.BlockSpec((S, hpb * dv), lambda b, h, i: (b, vcol // hpb + h)),
    ]
    args = [q, k, v]
    if decay:
        in_specs += [
            pl.BlockSpec((tq, hpb * LANES), lambda b, h, i: (b * nq + i, h)),
            pl.BlockSpec((1, hpb, 1, S), lambda b, h, i: (b, h, 0, 0)),
        ]
        args += [cum_col, cum_row]
    kern = functools.partial(_flash_kernel, tq=tq, tk=tk, rc=rc, scale=scale, decay=decay, hpb=hpb, dk=dk, dv=dv)
    R = hpb * tq
    return pl.pallas_call(
        kern,
        grid=(B, H // hpb, nq),
        in_specs=in_specs,
        out_specs=pl.BlockSpec((tq, hpb * dv), lambda b, h, i: (b * nq + i, h)),
        out_shape=jax.ShapeDtypeStruct((N, H * dv), f32),
        scratch_shapes=[
            pltpu.VMEM((2, R, tk), f32),
            pltpu.VMEM((R, tk), _MXU_DTYPE),
            pltpu.VMEM((R, LANES), f32),
            pltpu.VMEM((R, LANES), f32),
            pltpu.VMEM((R, 2 * LANES), f32),
        ],
        compiler_params=_cparams(("arbitrary", "arbitrary", "arbitrary")),
        name="flash_fox" if decay else "flash_mla",
    )(*args)


def _fox_prep_kernel(ff_ref, b_ref, o_ref, rep_ref, *, chunk):
    S = ff_ref.shape[0]
    r = lax.broadcasted_iota(jnp.int32, (chunk, chunk), 0)
    c = lax.broadcasted_iota(jnp.int32, (chunk, chunk), 1)
    tri = (c <= r).astype(_MXU_DTYPE)

    def body(j, carry):
        r0 = pl.multiple_of(j * chunk, chunk)
        x = ff_ref[pl.ds(r0, chunk), :] + b_ref[...]
        lf = jnp.minimum(x, 0.0) - jnp.log(1.0 + jnp.exp(-jnp.abs(x)))
        hi, mid, lo = _split3(lf)
        cum = (_mm(tri, hi) + _mm(tri, mid)) + _mm(tri, lo) + carry
        o_ref[pl.ds(r0, chunk), :] = cum
        for h in range(FOX_HEADS):
            rep_ref[pl.ds(r0, chunk), h * LANES:(h + 1) * LANES] = jnp.broadcast_to(cum[:, h:h + 1], (chunk, LANES))
        return cum[chunk - 1:chunk, :]

    lax.fori_loop(0, S // chunk, body, jnp.zeros((1, LANES), f32))


def _fox_prep(p, b_pad, B, S):
    N = B * S
    kern = functools.partial(_fox_prep_kernel, chunk=_pick(S, 128, 8))
    return pl.pallas_call(
        kern,
        grid=(B,),
        in_specs=[
            pl.BlockSpec((S, LANES), lambda b: (b, (COL_FF - N_COLS16) // LANES)),
            pl.BlockSpec((1, LANES), lambda b: (0, 0)),
        ],
        out_specs=[pl.BlockSpec((S, LANES), lambda b: (b, 0)),
                   pl.BlockSpec((S, FOX_HEADS * LANES), lambda b: (b, 0))],
        out_shape=[jax.ShapeDtypeStruct((N, LANES), f32),
                   jax.ShapeDtypeStruct((N, FOX_HEADS * LANES), f32)],
        compiler_params=_cparams(("arbitrary",)),
        name="fox_prep",
    )(p, b_pad)


def _nsa_prep_kernel(q_ref, kc_ref, vc_ref, ks_ref, vs_ref, kw_ref, vw_ref, pos_ref, invf_ref, sign_ref, blk_ref,
                     qo_ref, cmp_ref, kso_ref, vso_ref, kwo_ref, vwo_ref, kc_scr, *, scale):
    cos_t, sin_t = _rot_tables(pos_ref, invf_ref, sign_ref)
    half = PARTIAL_ROT // 2
    rot = lambda x: _rotate(x.astype(f32), cos_t, sin_t, half, LANES)
    for h in range(NSA_HEADS):
        sl = slice(h * HEAD_DIM, (h + 1) * HEAD_DIM)
        qo_ref[:, sl] = (rot(q_ref[:, sl]) * scale).astype(qo_ref.dtype)
    nrow = cmp_ref.shape[2]
    for g in range(NSA_KV_GROUPS):
        sl = slice(g * HEAD_DIM, (g + 1) * HEAD_DIM)
        kso_ref[:, 2 * g * HEAD_DIM:(2 * g + 1) * HEAD_DIM] = rot(ks_ref[:, sl]).astype(kso_ref.dtype)
        kso_ref[:, (2 * g + 1) * HEAD_DIM:(2 * g + 2) * HEAD_DIM] = blk_ref[...]
        kwo_ref[:, sl] = rot(kw_ref[:, sl]).astype(kwo_ref.dtype)
        kc_scr[0] = rot(kc_ref[:, sl])
        kc_scr[1] = vc_ref[:, sl].astype(f32)
        for t in range(NSA_CMP_STRIDE):
            cs = slice(t * HEAD_DIM, (t + 1) * HEAD_DIM)
            for kv in range(2):
                cmp_ref[kv, g, :, cs] = kc_scr[kv, pl.ds(t, nrow, stride=NSA_CMP_STRIDE), :].astype(cmp_ref.dtype)
    vso_ref[...] = vs_ref[...].astype(vso_ref.dtype)
    vwo_ref[...] = vw_ref[...].astype(vwo_ref.dtype)


def _nsa_prep(p, p32, pos, invf, sign, expand, S):
    N = p.shape[0]
    tm = _pick(S, 512, 8)
    nt = S // tm
    G = NSA_KV_GROUPS
    cw = NSA_CMP_STRIDE * HEAD_DIM
    kv = lambda col: pl.BlockSpec((tm, NSA_KV), lambda i: (i, col // NSA_KV))
    row = lambda i: (0, 0)
    kern = functools.partial(_nsa_prep_kernel, scale=HEAD_DIM ** -0.5)
    return pl.pallas_call(
        kern,
        grid=(N // tm,),
        in_specs=[
            pl.BlockSpec((tm, W_NSA), lambda i: (i, COL_NQ // W_NSA)),
            kv(COL_NKC), kv(COL_NVC), kv(COL_NKS), kv(COL_NVS), kv(COL_NKW), kv(COL_NVW - N_COLS16),
            pl.BlockSpec((tm, 1), lambda i: (i, 0)),
            pl.BlockSpec((1, LANES), row),
            pl.BlockSpec((1, LANES), row),
            pl.BlockSpec((tm, LANES), lambda i: (i % nt, 0)),
        ],
        out_specs=[
            pl.BlockSpec((tm, W_NSA), lambda i: (i, 0)),
            pl.BlockSpec((2, G, tm // NSA_CMP_STRIDE, cw), lambda i: (0, 0, i, 0)),
            pl.BlockSpec((tm, 2 * NSA_KV), lambda i: (i, 0)),
            pl.BlockSpec((tm, NSA_KV), lambda i: (i, 0)),
            pl.BlockSpec((tm, NSA_KV), lambda i: (i, 0)),
            pl.BlockSpec((tm, NSA_KV), lambda i: (i, 0)),
        ],
        out_shape=[
            jax.ShapeDtypeStruct((N, W_NSA), _MXU_DTYPE),
            jax.ShapeDtypeStruct((2, G, N // NSA_CMP_STRIDE, cw), _MXU_DTYPE),
            jax.ShapeDtypeStruct((N, 2 * NSA_KV), _MXU_DTYPE),
            jax.ShapeDtypeStruct((N, NSA_KV), _MXU_DTYPE),
            jax.ShapeDtypeStruct((N, NSA_KV), _MXU_DTYPE),
            jax.ShapeDtypeStruct((N, NSA_KV), _MXU_DTYPE),
        ],
        scratch_shapes=[pltpu.VMEM((2, tm, HEAD_DIM), f32)],
        compiler_params=_cparams(("arbitrary",)),
        name="nsa_prep",
    )(p, p, p, p, p, p, p32, pos, invf, sign, expand)


def _compress_kernel(c_ref, pos_ref, w1_ref, w2_ref, o_ref):
    x = c_ref[0, 0, 0]
    half = x.shape[1]
    nrow = x.shape[0]
    a = _mm(x, w1_ref[0, :half, :])
    b = _mm(x, w1_ref[0, half:, :])
    posb = _mm(pos_ref[0].astype(_MXU_DTYPE), w1_ref[0])[0:1, :]
    hid = a + pltpu.roll(b, nrow - 1, 0) + posb
    c0 = math.sqrt(2.0 / math.pi)
    act = 0.5 * hid * (1.0 + jnp.tanh(c0 * (hid + 0.044715 * (hid * hid * hid))))
    o_ref[0, 0, 0] = _mm(act.astype(_MXU_DTYPE), w2_ref[0])


def _compress(cmp_in, pos_flat, w1, w2, B, S):
    G = NSA_KV_GROUPS
    nch = S // NSA_CMP_STRIDE
    width = NSA_CMP_STRIDE * HEAD_DIM
    x = cmp_in.reshape(2, G, B, nch, width)
    return pl.pallas_call(
        _compress_kernel,
        grid=(2, G, B),
        in_specs=[
            pl.BlockSpec((1, 1, 1, nch, width), lambda t, g, b: (t, g, b, 0, 0)),
            pl.BlockSpec((1, 8, 2 * width), lambda t, g, b: (t, 0, 0)),
            pl.BlockSpec((1, 2 * width, NSA_CMP_HIDDEN), lambda t, g, b: (t, 0, 0)),
            pl.BlockSpec((1, NSA_CMP_HIDDEN, HEAD_DIM), lambda t, g, b: (t, 0, 0)),
        ],
        out_specs=pl.BlockSpec((1, 1, 1, nch, HEAD_DIM), lambda t, g, b: (t, g, b, 0, 0)),
        out_shape=jax.ShapeDtypeStruct((2, G, B, nch, HEAD_DIM), f32),
        compiler_params=_cparams(("arbitrary", "arbitrary", "arbitrary")),
        name="nsa_compress",
    )(x, pos_flat, w1, w2)


def _nsa_attn_kernel(q_ref, kc_ref, vc_ref, ks_ref, vs_ref, kw_ref, vw_ref, g_ref, ovl_ref, o_ref,
                     s_ref, p_ref, m_ref, a_ref, acc_ref, out_scr, psum_ref,
                     *, tq, tks, tkw, rc, n_slc, top_k):
    i = pl.program_id(2)
    Hg = NSA_GQA
    D = HEAD_DIM
    q4 = q_ref[...]
    qs = jnp.concatenate([q4[:, h * D:(h + 1) * D] for h in range(Hg)], axis=0)
    t = i * tq + lax.broadcasted_iota(jnp.int32, (tq, 1), 0)
    gate = jax.nn.sigmoid(g_ref[...])
    head_rows = lambda h: slice(h * tq, (h + 1) * tq)

    kc = kc_ref[0, 0, 0].astype(_MXU_DTYPE)
    vc = vc_ref[0, 0, 0].astype(_MXU_DTYPE)
    ncmp = kc.shape[0]
    cend = lax.broadcasted_iota(jnp.int32, (1, ncmp), 1) * NSA_CMP_STRIDE + (NSA_CMP_LEN - 1)
    s_ref[0, :, :ncmp] = _mm_nt(qs, kc)
    cq = LANES
    for h in range(Hg):
        for q0 in range(0, tq, cq):
            rs = slice(h * tq + q0, h * tq + q0 + cq)
            tc = t[q0:q0 + cq]
            s = jnp.where(cend <= tc, s_ref[0, rs, :ncmp], NEG)
            e = jnp.exp(s - jnp.max(s, axis=1, keepdims=True))
            l = jnp.sum(e, axis=1, keepdims=True)
            p = e * jnp.where(tc >= NSA_CMP_LEN - 1, 1.0 / l, 0.0)
            p_ref[rs, :ncmp] = p.astype(p_ref.dtype)
            if h == 0:
                psum_ref[q0:q0 + cq] = p
            else:
                psum_ref[q0:q0 + cq] += p
    o_cmp = _mm(p_ref[:, :ncmp], vc)
    for h in range(Hg):
        out_scr[head_rows(h)] = gate[:, 3 * h:3 * h + 1] * o_cmp[head_rows(h)]

    hi, mid, lo = _split3(psum_ref[...])
    ovl = ovl_ref[...]
    imp = (_mm(hi, ovl) + _mm(mid, ovl)) + _mm(lo, ovl)
    nb = -(-n_slc // 8) * 8
    x = imp.T[:nb]
    jr = lax.broadcasted_iota(jnp.int32, (nb, 1), 0)
    cur_l = (i * tq + lax.broadcasted_iota(jnp.int32, (1, tq), 1)) // NSA_SLC_BLOCK
    forced = (jr == 0) | (jr == cur_l) | (jr == cur_l - 1)
    x = jnp.where(forced, BIG, jnp.where(jr > cur_l, NEG, x))
    xg = [x[g0:g0 + 8] for g0 in range(0, nb, 8)]
    jg = lax.broadcasted_iota(jnp.int32, (8, 1), 0)
    rg = [jnp.zeros((8, tq), f32) for _ in xg]
    for kk in range(n_slc):
        ck = jnp.broadcast_to(x[kk:kk + 1, :], (8, tq))
        for g, xv in enumerate(xg):
            if 8 * g > kk:
                before = ck >= xv
            elif 8 * g + 7 < kk:
                before = ck > xv
            else:
                before = (ck > xv) | ((ck == xv) & (jg + 8 * g > kk))
            rg[g] = rg[g] + before.astype(f32)
    rank = jnp.concatenate(rg, axis=0)
    unsel_t = jnp.where((rank < float(top_k)) & (jr <= cur_l), 0.0, NEG)
    if nb < LANES:
        unsel_t = jnp.concatenate([unsel_t, jnp.zeros((LANES - nb, tq), f32)], axis=0)
    unsel = unsel_t.T.astype(_MXU_DTYPE)
    q_aug = jnp.concatenate([qs, jnp.concatenate([unsel] * Hg, axis=0)], axis=1)

    def store_masked(slot, s, allowed, width):
        bias = jnp.where(allowed, 0.0, NEG)
        for h in range(Hg):
            s_ref[slot, head_rows(h), :width] = s[head_rows(h)] + bias

    def add_gated(c):
        o = _softmax_result(acc_ref)
        for h in range(Hg):
            out_scr[head_rows(h)] += gate[:, 3 * h + c:3 * h + c + 1] * o[head_rows(h)]

    n_ts = ((i + 1) * tq + tks - 1) // tks

    def slc_produce(kt, slot):
        k0 = pl.multiple_of(kt * tks, tks)
        s_ref[slot] = _mm_nt(q_aug, ks_ref[pl.ds(k0, tks), :])

    def slc_mask_last(slot):
        col = (n_ts - 1) * tks + lax.broadcasted_iota(jnp.int32, (1, tks), 1)
        for r0 in range(0, Hg * tq, rc):
            q0 = r0 % tq
            s_ref[slot, r0:r0 + rc, :] = jnp.where(col <= t[q0:q0 + rc], s_ref[slot, r0:r0 + rc, :], NEG)

    def slc_consume(kt, slot):
        k0 = pl.multiple_of(kt * tks, tks)
        _softmax_step(s_ref, slot, tks, vs_ref[pl.ds(k0, tks), :], m_ref, a_ref, p_ref, acc_ref, rc)

    _softmax_init(m_ref, acc_ref)
    _tile_pipeline(n_ts, slc_produce, slc_consume, slc_mask_last)
    add_gated(1)

    hi_t = (i * tq + tq - 1) // tkw
    lo_t = jnp.maximum(i * tq - (NSA_WINDOW - 1), 0) // tkw

    def win_produce(j, slot):
        k0 = pl.multiple_of((hi_t - j) * tkw, tkw)
        col = k0 + lax.broadcasted_iota(jnp.int32, (1, tkw), 1)
        store_masked(slot, _mm_nt(qs, kw_ref[pl.ds(k0, tkw), :]), (col <= t) & (t - col < NSA_WINDOW), tkw)

    def win_consume(j, slot):
        k0 = pl.multiple_of((hi_t - j) * tkw, tkw)
        _softmax_step(s_ref, slot, tkw, vw_ref[pl.ds(k0, tkw), :], m_ref, a_ref, p_ref, acc_ref,
                      rc * (tks // tkw))

    _softmax_init(m_ref, acc_ref)
    _tile_pipeline(hi_t - lo_t + 1, win_produce, win_consume)
    add_gated(2)
    for h in range(Hg):
        o_ref[:, h * D:(h + 1) * D] = out_scr[head_rows(h)].astype(o_ref.dtype)


def _nsa_attn(qn, cmp_kv, ks, vs, kw, vw, p, ovl, B, S):
    N = B * S
    G = NSA_KV_GROUPS
    D = HEAD_DIM
    tq = _pick(S, 256, LANES)
    tks = _pick(S, 512, LANES)
    tkw = _pick(S, 256, LANES)
    rc = _pick(tq, 64, 8)
    nq = S // tq
    nch = cmp_kv.shape[3]
    n_slc = S // NSA_SLC_BLOCK
    assert n_slc <= LANES and nch <= tks and tkw <= tks
    kern = functools.partial(_nsa_attn_kernel, tq=tq, tks=tks, tkw=tkw, rc=rc, n_slc=n_slc,
                             top_k=min(NSA_TOPK, n_slc))
    R = NSA_GQA * tq
    tk = tks
    seq = lambda: pl.BlockSpec((S, D), lambda b, g, i: (b, g))
    return pl.pallas_call(
        kern,
        grid=(B, G, nq),
        in_specs=[
            pl.BlockSpec((tq, NSA_GQA * D), lambda b, g, i: (b * nq + i, g)),
            pl.BlockSpec((1, 1, 1, nch, D), lambda b, g, i: (0, g, b, 0, 0)),
            pl.BlockSpec((1, 1, 1, nch, D), lambda b, g, i: (1, g, b, 0, 0)),
            pl.BlockSpec((S, 2 * D), lambda b, g, i: (b, g)),
            seq(), seq(), seq(),
            pl.BlockSpec((tq, LANES), lambda b, g, i: (b * nq + i, (COL_NG - N_COLS16) // LANES + g)),
            pl.BlockSpec(ovl.shape, lambda b, g, i: (0, 0)),
        ],
        out_specs=pl.BlockSpec((tq, NSA_GQA * D), lambda b, g, i: (b * nq + i, g)),
        out_shape=jax.ShapeDtypeStruct((N, W_NSA), f32),
        scratch_shapes=[
            pltpu.VMEM((2, R, tk), f32),
            pltpu.VMEM((R, tk), _MXU_DTYPE),
            pltpu.VMEM((R, LANES), f32),
            pltpu.VMEM((R, LANES), f32),
            pltpu.VMEM((R, 2 * D), f32),
            pltpu.VMEM((R, D), f32),
            pltpu.VMEM((tq, nch), f32),
        ],
        compiler_params=_cparams(("arbitrary", "arbitrary", "arbitrary")),
        name="nsa_attn",
    )(qn, cmp_kv, cmp_kv, ks, vs, kw, vw, p, ovl)


def _nsa_consts(S):
    nch = S // NSA_CMP_STRIDE
    n_slc = S // NSA_SLC_BLOCK
    c_start = np.arange(nch) * NSA_CMP_STRIDE
    s_start = np.arange(n_slc) * NSA_SLC_BLOCK
    ov = (np.minimum(c_start[:, None] + NSA_CMP_LEN, s_start[None, :] + NSA_SLC_BLOCK)
          - np.maximum(c_start[:, None], s_start[None, :]))
    ovl = np.zeros((nch, LANES), np.float32)
    ovl[:, :n_slc] = np.clip(ov, 0, None) / NSA_CMP_LEN
    expand = np.zeros((S, LANES), np.float32)
    expand[np.arange(S), np.arange(S) // NSA_SLC_BLOCK] = 1.0
    return jnp.asarray(ovl, _MXU_DTYPE), jnp.asarray(expand, _MXU_DTYPE)


def _out_proj_kernel(om_ref, of_ref, on_ref, gm_ref, gf_ref, gn_ref, w_ref, x_ref, gt_ref, o_ref, h_ref):
    @pl.when(pl.program_id(1) == 0)
    def _():
        h_ref[:, :W_MLA] = _rms(om_ref[...].astype(f32), gm_ref[...]).astype(h_ref.dtype)
        h_ref[:, W_MLA:W_MLA + W_FOX] = _rms(of_ref[...].astype(f32), gf_ref[...]).astype(h_ref.dtype)
        h_ref[:, W_MLA + W_FOX:] = _rms(on_ref[...].astype(f32), gn_ref[...]).astype(h_ref.dtype)

    o_ref[...] = x_ref[...] + gt_ref[0] * _mm(h_ref[...], w_ref[0])


def _out_proj(o_mla, o_fox, o_nsa, gm, gf, gn, w, l, x2, mod3, S):
    N, D = x2.shape
    DM = w.shape[1]
    tm = _pick(S, 512, 8)
    tn = _pick(D, 2048)
    nj = D // tn
    row = lambda i, j: (0, 0)
    return pl.pallas_call(
        _out_proj_kernel,
        grid=(N // tm, nj),
        in_specs=[
            pl.BlockSpec((tm, W_MLA), lambda i, j: (i, 0)),
            pl.BlockSpec((tm, W_FOX), lambda i, j: (i, 0)),
            pl.BlockSpec((tm, W_NSA), lambda i, j: (i, 0)),
            pl.BlockSpec((1, W_MLA), row),
            pl.BlockSpec((1, W_FOX), row),
            pl.BlockSpec((1, W_NSA), row),
            pl.BlockSpec((1, DM, tn), lambda i, j: (l, 0, j)),
            pl.BlockSpec((tm, tn), lambda i, j: (i, j)),
            pl.BlockSpec((1, 1, tn), lambda i, j: (i * tm // S, 0, 2 * nj + j)),
        ],
        out_specs=pl.BlockSpec((tm, tn), lambda i, j: (i, j)),
        out_shape=jax.ShapeDtypeStruct((N, D), f32),
        scratch_shapes=[pltpu.VMEM((tm, DM), _MXU_DTYPE)],
        compiler_params=_cparams(("arbitrary", "arbitrary")),
        name="out_proj",
    )(o_mla, o_fox, o_nsa, gm, gf, gn, w, x2, mod3)


def _mlp_kernel(x_ref, sh_ref, sc_ref, gt_ref, g_ref, w1_ref, w2_ref, fg_ref, o_ref, h_ref, acc_ref, *, final):
    f = pl.program_id(1)

    @pl.when(f == 0)
    def _():
        h = _rms(x_ref[...], g_ref[...]) * (1.0 + sc_ref[0]) + sh_ref[0]
        h_ref[...] = h.astype(h_ref.dtype)
        acc_ref[...] = jnp.zeros_like(acc_ref)

    u = jnp.maximum(_mm(h_ref[...], w1_ref[0]), 0.0)
    acc_ref[...] += _mm((u * u).astype(_MXU_DTYPE), w2_ref[0])

    @pl.when(f == pl.num_programs(1) - 1)
    def _():
        y = x_ref[...] + gt_ref[0] * acc_ref[...]
        if final:
            y = _rms(y, fg_ref[...])
        o_ref[...] = y


def _mlp(x2, mod3, gain, w1, w2, l, final_gain, S, final):
    N, D = x2.shape
    F = w1.shape[2]
    tm = _pick(S, 512, 8)
    tf = _pick(F, 1024)
    row = lambda i, f: (0, 0)
    mod_blk = lambda c: pl.BlockSpec((1, 1, D), lambda i, f: (i * tm // S, 0, c))
    return pl.pallas_call(
        functools.partial(_mlp_kernel, final=final),
        grid=(N // tm, F // tf),
        in_specs=[
            pl.BlockSpec((tm, D), lambda i, f: (i, 0)),
            mod_blk(3), mod_blk(4), mod_blk(5),
            pl.BlockSpec((1, D), row),
            pl.BlockSpec((1, D, tf), lambda i, f: (l, 0, f)),
            pl.BlockSpec((1, tf, D), lambda i, f: (l, f, 0)),
            pl.BlockSpec((1, D), row),
        ],
        out_specs=pl.BlockSpec((tm, D), lambda i, f: (i, 0)),
        out_shape=jax.ShapeDtypeStruct((N, D), f32),
        scratch_shapes=[pltpu.VMEM((tm, D), _MXU_DTYPE), pltpu.VMEM((tm, D), f32)],
        compiler_params=_cparams(("arbitrary", "arbitrary")),
        name="mlp",
    )(x2, mod3, mod3, mod3, gain, w1, w2, final_gain)


def _w_in_segments():
    widths = (MLA_Q_LORA, MLA_KV_LORA, MLA_ROPE, W_FOX, W_FOX, W_FOX, FOX_HEADS,
              W_NSA, NSA_KV, NSA_KV, NSA_KV, NSA_KV, NSA_KV, NSA_KV, 3 * NSA_HEADS)
    src = np.cumsum((0,) + widths)
    (cq, ckv, kr, fq, fk, fv, ff, nq, nkc, nvc, nks, nvs, nkw, nvw, ng) = [int(s) for s in src[:-1]]
    gw = 3 * NSA_GQA
    return [(COL_NQ, nq, W_NSA), (COL_CQ, cq, MLA_Q_LORA), (COL_FQ, fq, W_FOX), (COL_FK, fk, W_FOX),
            (COL_FV, fv, W_FOX), (COL_CKV, ckv, MLA_KV_LORA), (COL_NKC, nkc, NSA_KV), (COL_NVC, nvc, NSA_KV),
            (COL_NKS, nks, NSA_KV), (COL_NVS, nvs, NSA_KV), (COL_NKW, nkw, NSA_KV), (COL_NVW, nvw, NSA_KV),
            (COL_KR, kr, MLA_ROPE), (COL_FF, ff, FOX_HEADS), (COL_NG, ng, gw), (COL_NG + LANES, ng + gw, gw)]


def _w_in_regroup_kernel(w_ref, o_ref):
    o_ref[...] = jnp.zeros(o_ref.shape, o_ref.dtype)
    for dst, src, width in _w_in_segments():
        o_ref[:, dst:dst + width] = w_ref[0, :, src:src + width].astype(o_ref.dtype)


def _prep_w_in(w_in, l):
    _, D, C = w_in.shape
    tr = _pick(D, 256, 8)
    return pl.pallas_call(
        _w_in_regroup_kernel,
        grid=(D // tr,),
        in_specs=[pl.BlockSpec((1, tr, C), lambda i: (l, i, 0))],
        out_specs=pl.BlockSpec((tr, N_COLS), lambda i: (i, 0)),
        out_shape=jax.ShapeDtypeStruct((D, N_COLS), _MXU_DTYPE),
        compiler_params=_cparams(("arbitrary",)),
        name="w_in_regroup",
    )(w_in)


def _prep_mla_w(w_uq, w_ukv):
    H = MLA_HEADS
    qd = MLA_NOPE + MLA_ROPE
    q_nope = np.concatenate([np.arange(h * qd, h * qd + MLA_NOPE) for h in range(H)])
    q_rope = np.concatenate([np.arange(h * qd + MLA_NOPE, (h + 1) * qd) for h in range(H)])
    kd = MLA_NOPE + MLA_V
    k_nope = np.concatenate([np.arange(h * kd, h * kd + MLA_NOPE) for h in range(H)])
    v_cols = np.concatenate([np.arange(h * kd + MLA_NOPE, (h + 1) * kd) for h in range(H)])
    wq = jnp.take(w_uq, jnp.asarray(np.concatenate([q_nope, q_rope])), axis=1).astype(_MXU_DTYPE)
    wkv = jnp.take(w_ukv, jnp.asarray(np.concatenate([k_nope, v_cols])), axis=1).astype(_MXU_DTYPE)
    return wq, wkv


def _pad_lanes(v):
    return jnp.zeros((1, LANES), f32).at[0, :v.shape[0]].set(v.astype(f32))


def kernel(x, c, positions, ada_w, ada_b, norm_mix, norm_mlp, w_in, fox_b_f, mla_q_norm, mla_w_uq, mla_kv_norm,
           mla_w_ukv, nsa_pos_k, nsa_pos_v, nsa_cmp_k_w1, nsa_cmp_k_w2, nsa_cmp_v_w1, nsa_cmp_v_w2, out_norm,
           w_out, mlp_w1, mlp_w2, final_norm):
    B, S, D = x.shape
    L = ada_w.shape[0]
    N = B * S
    assert S % LANES == 0 and D % LANES == 0 and B <= 8

    x2 = x.reshape(N, D)
    pos = positions.reshape(N, 1).astype(jnp.int32)
    c_pad = jnp.zeros((8, D), f32).at[:B].set(c)
    mod = _ada_mod(c_pad, ada_w, ada_b)

    invf_mla, sign_mla = _rope_consts(MLA_ROPE // 2, MLA_ROPE, LANES)
    invf_nsa, sign_nsa = _rope_consts(PARTIAL_ROT // 2, LANES, LANES)
    ovl, expand = _nsa_consts(S)
    row = lambda v: v.reshape(1, -1)
    w_out_c = w_out.astype(_MXU_DTYPE)
    w1_c = mlp_w1.astype(_MXU_DTYPE)
    w2_c = mlp_w2.astype(_MXU_DTYPE)

    for l in range(L):
        mod3 = mod[l, :B].reshape(B, 1, 6 * D)
        p, p32 = _proj_in(x2, mod3, row(norm_mix[l]), _prep_w_in(w_in, l), S)

        wq, wkv = _prep_mla_w(mla_w_uq[l], mla_w_ukv[l])
        q_m, k_m, v_m = _mla_up(p, p32, pos, invf_mla, sign_mla, row(mla_q_norm[l]), row(mla_kv_norm[l]),
                                wq, wkv, S)
        o_mla = _flash(q_m, k_m, v_m, B=B, S=S, H=MLA_HEADS, dk=MLA_QK, dv=MLA_V, qcol=0, kcol=0, vcol=0,
                       scale=1.0)

        cum, cum_rep = _fox_prep(p32, _pad_lanes(fox_b_f[l]), B, S)
        cum_row = cum[:, :FOX_HEADS].reshape(B, S, FOX_HEADS).transpose(0, 2, 1)[:, :, None, :]
        o_fox = _flash(p, p, p, B=B, S=S, H=FOX_HEADS, dk=HEAD_DIM, dv=HEAD_DIM,
                       qcol=COL_FQ // HEAD_DIM, kcol=COL_FK // HEAD_DIM, vcol=COL_FV // HEAD_DIM,
                       scale=HEAD_DIM ** -0.5, cum_col=cum_rep, cum_row=cum_row)

        qn, cmp_in, ks, vs, kw, vw = _nsa_prep(p, p32, pos, invf_nsa, sign_nsa, expand, S)
        flat = lambda a: jnp.broadcast_to(a.reshape(1, -1), (8, a.size))
        cmp_kv = _compress(
            cmp_in,
            jnp.stack([flat(nsa_pos_k[l]), flat(nsa_pos_v[l])]).astype(f32),
            jnp.stack([nsa_cmp_k_w1[l], nsa_cmp_v_w1[l]]).astype(_MXU_DTYPE),
            jnp.stack([nsa_cmp_k_w2[l], nsa_cmp_v_w2[l]]).astype(_MXU_DTYPE), B, S)
        o_nsa = _nsa_attn(qn, cmp_kv, ks, vs, kw, vw, p32, ovl, B, S)

        gn = out_norm[l]
        x2 = _out_proj(o_mla, o_fox, o_nsa, row(gn[:W_MLA]), row(gn[W_MLA:W_MLA + W_FOX]),
                       row(gn[W_MLA + W_FOX:]), w_out_c, l, x2, mod3, S)
        x2 = _mlp(x2, mod3, row(norm_mlp[l]), w1_c, w2_c, l, row(final_norm), S, final=(l == L - 1))
    return x2.reshape(B, S, D)
```
